```python
import math
import jax, jax.numpy as jnp
from jax import lax
import numpy as np

D_MODEL = 2048
BATCH = 2
SEQ = 8192
DEPTH = 1


GRID_W = 64
N_Q_HEADS = 8
N_KV_HEADS = 2
HEAD_DIM = 128
ATTN_WIDTH = N_Q_HEADS * HEAD_DIM
KV_WIDTH = N_KV_HEADS * HEAD_DIM
ROPE_THETA = 10000.0
Q_BLOCK = 128
QK_EPS = 1e-6
HYENA_WIDTH = D_MODEL // 2
HYENA_GROUPS = 8
SHORT_CONV = 3
FILTER_BANDS = 16
FILTER_EMB = 1 + 2 * FILTER_BANDS
FILTER_HIDDEN = 64
DECAY_TARGET = 1e-2
FAST_DECAY_PCT = 0.3
SLOW_DECAY_PCT = 1.5
FILTER_EPS = 1e-6
N_BRANCH = 2
IN_WIDTH = ATTN_WIDTH + 2 * KV_WIDTH + 3 * HYENA_WIDTH + N_BRANCH * D_MODEL
N_GROUPS = 4
EXPERTS_PER_GROUP = 8
N_EXPERTS = N_GROUPS * EXPERTS_PER_GROUP
TOP_K = 2
D_EXPERT = D_MODEL // 2
MOE_BLOCK = 128
DEEPNORM_ALPHA = (2 * DEPTH) ** 0.25
DEEPNORM_BETA = (8 * DEPTH) ** -0.25
LN_EPS = 1e-5

kernel_name = 'hybrid_gqa_hyena_hmoe_deepnorm_encoder'


def layer_norm(x, g, b):
    xf = x.astype(jnp.float32)
    mu = jnp.mean(xf, axis=-1, keepdims=True)
    var = jnp.mean(jnp.square(xf - mu), axis=-1, keepdims=True)
    return ((xf - mu) * lax.rsqrt(var + LN_EPS) * g.astype(jnp.float32) + b.astype(jnp.float32)).astype(x.dtype)


def rms_norm_heads(x, g):
    xf = x.astype(jnp.float32)
    xf = xf * lax.rsqrt(jnp.mean(jnp.square(xf), axis=-1, keepdims=True) + QK_EPS)
    return (xf * g.astype(jnp.float32)).astype(x.dtype)


def axial_rope_tables(seq_len):
    rows = seq_len // GRID_W
    row_idx = jnp.repeat(jnp.arange(rows, dtype=jnp.int32), GRID_W).astype(jnp.float32)
    col_idx = jnp.tile(jnp.arange(GRID_W, dtype=jnp.int32), rows).astype(jnp.float32)
    half = HEAD_DIM // 2
    inv_freq = ROPE_THETA ** (-jnp.arange(0, half, 2, dtype=jnp.float32) / half)
    ang_r = row_idx[:, None] * inv_freq[None, :]
    ang_c = col_idx[:, None] * inv_freq[None, :]
    return jnp.cos(ang_r), jnp.sin(ang_r), jnp.cos(ang_c), jnp.sin(ang_c)


def rotate_half(x, cos, sin):
    d2 = x.shape[-1] // 2
    x1, x2 = x[..., :d2], x[..., d2:]
    c = cos[None, :, None, :]
    s = sin[None, :, None, :]
    return jnp.concatenate([x1 * c - x2 * s, x2 * c + x1 * s], axis=-1)


def apply_axial_rope(x, tables):
    cr, sr, cc, sc = tables
    xf = x.astype(jnp.float32)
    half = HEAD_DIM // 2
    out = jnp.concatenate([rotate_half(xf[..., :half], cr, sr),
                           rotate_half(xf[..., half:], cc, sc)], axis=-1)
    return out.astype(x.dtype)


def block_attention(q, k, v):
    b, s, _, _ = q.shape
    n_blk = s // Q_BLOCK
    grp = N_Q_HEADS // N_KV_HEADS
    qb = q.reshape(b, n_blk, Q_BLOCK, N_KV_HEADS, grp, HEAD_DIM).transpose(1, 0, 2, 3, 4, 5)
    scale = HEAD_DIM ** -0.5

    def one_block(q_blk):
        sc = jnp.einsum('bqkgd,bskd->bkgqs', q_blk, k).astype(jnp.float32) * scale
        p = jax.nn.softmax(sc, axis=-1).astype(v.dtype)
        return jnp.einsum('bkgqs,bskd->bqkgd', p, v)

    o = lax.map(one_block, qb)
    return o.transpose(1, 0, 2, 3, 4, 5).reshape(b, s, ATTN_WIDTH)


def short_conv(x, w, bias):
    s = x.shape[1]
    pad = SHORT_CONV // 2
    xp = jnp.pad(x, ((0, 0), (pad, pad), (0, 0)))
    out = bias
    for i in range(SHORT_CONV):
        out = out + xp[:, i:i + s] * w[i]
    return out


def hyena_filters(seq_len, w1, b1, f1, w2, b2, f2, w3):
    f32 = jnp.float32
    t = jnp.linspace(0.0, 1.0, seq_len, dtype=f32)[:, None]
    w = 2.0 * math.pi * jnp.arange(seq_len, dtype=f32)[:, None] / seq_len
    bands = jnp.linspace(1e-4, FILTER_BANDS - 1, FILTER_BANDS, dtype=f32)[None, :]
    feats = jnp.concatenate([t, jnp.cos(bands * w), -jnp.sin(bands * w)], axis=-1)
    h = jnp.sin(f1.astype(f32) * (feats @ w1.astype(f32) + b1.astype(f32)))
    h = jnp.sin(f2.astype(f32) * (h @ w2.astype(f32) + b2.astype(f32)))
    filt = h @ w3.astype(f32)
    min_decay = math.log(DECAY_TARGET) / SLOW_DECAY_PCT
    max_decay = math.log(DECAY_TARGET) / FAST_DECAY_PCT
    deltas = jnp.linspace(min_decay, max_decay, HYENA_WIDTH, dtype=f32)
    decay = jnp.exp(-t * jnp.abs(deltas)[None, :])
    h_fwd = filt[:, :HYENA_WIDTH] * decay
    h_bwd = filt[:, HYENA_WIDTH:] * decay
    two_sided = jnp.concatenate([h_fwd, jnp.zeros((1, HYENA_WIDTH), f32), h_bwd[:0:-1]], axis=0)
    two_sided = two_sided * lax.rsqrt(jnp.sum(jnp.square(two_sided), axis=0, keepdims=True) + FILTER_EPS)
    return jnp.fft.rfft(two_sided, axis=0)


def long_conv(z, filt_hat, d_bias):
    s = z.shape[1]
    zf = z.astype(jnp.float32)
    z_hat = jnp.fft.rfft(zf, n=2 * s, axis=1)
    y = jnp.fft.irfft(z_hat * filt_hat[None], n=2 * s, axis=1)[:, :s]
    return (y + zf * d_bias.astype(jnp.float32)).astype(z.dtype)


def hier_moe(x, w_rg, b_rg, w_re, b_re, w_gate, w_up, w_down):
    b, s, d = x.shape
    n_tok = b * s
    xt = x.reshape(n_tok, d)
    coarse = (xt @ w_rg + b_rg).astype(jnp.float32)
    p_grp, grp = lax.top_k(jax.nn.softmax(coarse, axis=-1), 1)
    fine = (xt @ w_re + b_re).astype(jnp.float32).reshape(n_tok, N_GROUPS, EXPERTS_PER_GROUP)
    fine_sel = fine[jnp.arange(n_tok), grp[:, 0]]
    top_p, top_i = lax.top_k(jax.nn.softmax(fine_sel, axis=-1), TOP_K)
    gate = p_grp * top_p / jnp.sum(top_p, axis=-1, keepdims=True)
    expert = grp * EXPERTS_PER_GROUP + top_i

    n_assign = n_tok * TOP_K
    e_flat = expert.reshape(-1).astype(jnp.int32)
    tok_flat = jnp.repeat(jnp.arange(n_tok, dtype=jnp.int32), TOP_K)
    w_flat = gate.reshape(-1)
    order = jnp.argsort(e_flat)
    se, st, sw = e_flat[order], tok_flat[order], w_flat[order]
    counts = jnp.bincount(e_flat, length=N_EXPERTS)
    padded = (counts + MOE_BLOCK - 1) // MOE_BLOCK * MOE_BLOCK
    pad_end = jnp.cumsum(padded)
    pad_start = pad_end - padded
    start = jnp.cumsum(counts) - counts
    dest = pad_start[se] + jnp.arange(n_assign, dtype=jnp.int32) - start[se]
    n_rows = n_assign + N_EXPERTS * MOE_BLOCK
    n_blk = n_rows // MOE_BLOCK
    row_tok = jnp.zeros((n_rows,), jnp.int32).at[dest].set(st)
    row_w = jnp.zeros((n_rows,), jnp.float32).at[dest].set(sw)
    blk_start = jnp.arange(n_blk, dtype=jnp.int32) * MOE_BLOCK
    blk_expert = jnp.minimum(jnp.searchsorted(pad_end, blk_start, side='right'), N_EXPERTS - 1)
    xs = xt[row_tok].reshape(n_blk, MOE_BLOCK, d)

    def expert_block(args):
        xb, e = args
        hdn = jax.nn.silu(xb @ w_gate[e]) * (xb @ w_up[e])
        return hdn @ w_down[e]

    ys = lax.map(expert_block, (xs, blk_expert)).reshape(n_rows, d)
    out = jnp.zeros((n_tok, d), jnp.float32).at[row_tok].add(ys.astype(jnp.float32) * row_w[:, None])
    return out.astype(x.dtype).reshape(b, s, d)


def setup_inputs(seed: int = 0) -> dict:
    key = jax.random.key(seed)
    ks = iter(jax.random.split(key, 40))
    f32 = jnp.float32

    def nrm(shape, scale):
        return jax.random.normal(next(ks), shape, f32) * scale

    def gain(shape):
        return jnp.ones(shape, f32) + nrm(shape, 0.02)

    L_, D_, C_ = DEPTH, D_MODEL, HYENA_WIDTH
    return {
        'x': nrm((BATCH, SEQ, D_), 1.0),
        'ln_in_g': gain((D_,)),
        'ln_in_b': nrm((D_,), 0.02),
        'w_in': nrm((L_, D_, IN_WIDTH), D_ ** -0.5),
        'b_gate': nrm((L_, N_BRANCH * D_), 0.02),
        'q_norm_g': gain((L_, HEAD_DIM)),
        'k_norm_g': gain((L_, HEAD_DIM)),
        'hy_conv_w': nrm((L_, SHORT_CONV, 3 * C_), SHORT_CONV ** -0.5),
        'hy_conv_b': nrm((L_, 3 * C_), 0.02),
        'filt_w1': nrm((L_, FILTER_EMB, FILTER_HIDDEN), FILTER_EMB ** -0.5),
        'filt_b1': nrm((L_, FILTER_HIDDEN), 0.02),
        'filt_f1': gain((L_, FILTER_HIDDEN)),
        'filt_w2': nrm((L_, FILTER_HIDDEN, FILTER_HIDDEN), FILTER_HIDDEN ** -0.5),
        'filt_b2': nrm((L_, FILTER_HIDDEN), 0.02),
        'filt_f2': gain((L_, FILTER_HIDDEN)),
        'filt_w3': nrm((L_, FILTER_HIDDEN, 2 * C_), FILTER_HIDDEN ** -0.5),
        'hy_bias_d': nrm((L_, C_), 0.5),
        'w_attn_o': nrm((L_, ATTN_WIDTH, D_), ATTN_WIDTH ** -0.5),
        'w_hy_o': nrm((L_, C_, D_), C_ ** -0.5),
        'w_out': nrm((L_, D_, D_), D_ ** -0.5 * DEEPNORM_BETA),
        'ln1_g': gain((L_, D_)),
        'ln1_b': nrm((L_, D_), 0.02),
        'w_route_grp': nrm((L_, D_, N_GROUPS), D_ ** -0.5),
        'b_route_grp': nrm((L_, N_GROUPS), 0.01),
        'w_route_exp': nrm((L_, D_, N_EXPERTS), D_ ** -0.5),
        'b_route_exp': nrm((L_, N_EXPERTS), 0.01),
        'w_exp_gate': nrm((L_, N_EXPERTS, D_, D_EXPERT), D_ ** -0.5),
        'w_exp_up': nrm((L_, N_EXPERTS, D_, D_EXPERT), D_ ** -0.5),
        'w_exp_down': nrm((L_, N_EXPERTS, D_EXPERT, D_), D_EXPERT ** -0.5 * DEEPNORM_BETA),
        'ln2_g': gain((L_, D_)),
        'ln2_b': nrm((L_, D_), 0.02),
    }


def reference(x, ln_in_g, ln_in_b, w_in, b_gate, q_norm_g, k_norm_g, hy_conv_w, hy_conv_b,
              filt_w1, filt_b1, filt_f1, filt_w2, filt_b2, filt_f2, filt_w3, hy_bias_d,
              w_attn_o, w_hy_o, w_out, ln1_g, ln1_b, w_route_grp, b_route_grp, w_route_exp,
              b_route_exp, w_exp_gate, w_exp_up, w_exp_down, ln2_g, ln2_b):
    b, s, _ = x.shape
    rope = axial_rope_tables(s)
    split_at = [ATTN_WIDTH, ATTN_WIDTH + KV_WIDTH, ATTN_WIDTH + 2 * KV_WIDTH,
                ATTN_WIDTH + 2 * KV_WIDTH + 3 * HYENA_WIDTH]
    h = layer_norm(x, ln_in_g, ln_in_b)
    for l in range(DEPTH):
        proj = h @ w_in[l]
        q, k, v, hy, gates = jnp.split(proj, split_at, axis=-1)
        q = apply_axial_rope(rms_norm_heads(q.reshape(b, s, N_Q_HEADS, HEAD_DIM), q_norm_g[l]), rope)
        k = apply_axial_rope(rms_norm_heads(k.reshape(b, s, N_KV_HEADS, HEAD_DIM), k_norm_g[l]), rope)
        v = v.reshape(b, s, N_KV_HEADS, HEAD_DIM)
        y_attn = block_attention(q, k, v) @ w_attn_o[l]

        hy = short_conv(hy, hy_conv_w[l], hy_conv_b[l])
        x0, x1, hv = jnp.split(hy, 3, axis=-1)
        filt_hat = hyena_filters(s, filt_w1[l], filt_b1[l], filt_f1[l], filt_w2[l], filt_b2[l],
                                 filt_f2[l], filt_w3[l])
        y_hy = (x0 * long_conv(hv * x1, filt_hat, hy_bias_d[l])) @ w_hy_o[l]

        g_attn, g_hy = jnp.split(jax.nn.sigmoid(gates + b_gate[l]), N_BRANCH, axis=-1)
        mixed = (g_attn * y_attn + g_hy * y_hy) @ w_out[l]
        h = layer_norm(DEEPNORM_ALPHA * h + mixed, ln1_g[l], ln1_b[l])

        moe = hier_moe(h, w_route_grp[l], b_route_grp[l], w_route_exp[l], b_route_exp[l],
                       w_exp_gate[l], w_exp_up[l], w_exp_down[l])
        h = layer_norm(DEEPNORM_ALPHA * h + moe, ln2_g[l], ln2_b[l])
    return h
```

```python
import functools
import math

import jax
import jax.numpy as jnp
import numpy as np
from jax import lax
from jax.experimental import pallas as pl
from jax.experimental.pallas import tpu as pltpu

F32 = jnp.float32
BF16 = jnp.bfloat16

GRID_W = 64
N_Q_HEADS = 8
N_KV_HEADS = 2
HEAD_DIM = 128
Q_GROUP = N_Q_HEADS // N_KV_HEADS
ATTN_WIDTH = N_Q_HEADS * HEAD_DIM
KV_WIDTH = N_KV_HEADS * HEAD_DIM
ROPE_THETA = 10000.0
QK_EPS = 1e-6
SHORT_CONV = 3
FILTER_BANDS = 16
FILTER_EMB = 1 + 2 * FILTER_BANDS
DECAY_TARGET = 1e-2
FAST_DECAY_PCT = 0.3
SLOW_DECAY_PCT = 1.5
FILTER_EPS = 1e-6
N_GROUPS = 4
EXPERTS_PER_GROUP = 8
N_EXPERTS = N_GROUPS * EXPERTS_PER_GROUP
TOP_K = 2
LN_EPS = 1e-5
DEPTH = 1
DEEPNORM_ALPHA = (2 * DEPTH) ** 0.25

LANES = 128
V7X_VMEM_LIMIT = 56 * 1024 * 1024
DFT_MINOR = 128

MOE_ROWS = 256


def _params(sem, vmem=V7X_VMEM_LIMIT):
    return pltpu.CompilerParams(dimension_semantics=sem, vmem_limit_bytes=vmem)


def _layer_norm(x, g, b):
    mu = jnp.mean(x, axis=-1, keepdims=True)
    xc = x - mu
    var = jnp.mean(xc * xc, axis=-1, keepdims=True)
    return xc * lax.rsqrt(var + LN_EPS) * g + b


def _norm_rope(acc, gain, cs, sn):
    ms = jnp.mean(acc * acc, axis=-1, keepdims=True)
    xn = acc * lax.rsqrt(ms + QK_EPS) * gain
    lane = lax.broadcasted_iota(jnp.int32, xn.shape, 1)
    first = (lane % 64) < 32
    swapped = jnp.where(first, pltpu.roll(xn, 96, 1), pltpu.roll(xn, 32, 1))
    return xn * cs + swapped * sn


def _ln_inproj_kernel(x_ref, g_ref, b_ref, w_ref, qg_ref, kg_ref, cs_ref, sn_ref, o_ref, h_scr, *, tn):
    j = pl.program_id(1)
    nchunk = tn // HEAD_DIM
    n_q = ATTN_WIDTH // HEAD_DIM
    n_qk = n_q + N_KV_HEADS

    @pl.when(j == 0)
    def _():
        h = _layer_norm(x_ref[...], g_ref[...], b_ref[...])
        h_scr[...] = h.astype(BF16)

    acc = jnp.dot(h_scr[...], w_ref[...], preferred_element_type=F32)

    def store(first_plain_chunk_fn):
        for c in range(nchunk):
            sl = slice(c * HEAD_DIM, (c + 1) * HEAD_DIM)
            kind = first_plain_chunk_fn(c)
            if kind == "q":
                o_ref[:, sl] = _norm_rope(acc[:, sl], qg_ref[...], cs_ref[...], sn_ref[...]).astype(BF16)
            elif kind == "k":
                o_ref[:, sl] = _norm_rope(acc[:, sl], kg_ref[...], cs_ref[...], sn_ref[...]).astype(BF16)
            else:
                o_ref[:, sl] = acc[:, sl].astype(BF16)

    n_q_tiles = n_q // nchunk
    assert n_q % nchunk == 0 and N_KV_HEADS <= nchunk

    @pl.when(j < n_q_tiles)
    def _():
        store(lambda c: "q")

    @pl.when(j == n_q_tiles)
    def _():
        store(lambda c: "k" if c < N_KV_HEADS else "p")

    @pl.when(j > n_q_tiles)
    def _():
        store(lambda c: "p")


def _ln_inproj(x2, g, b, w_bf, qg, kg, cs, sn, seq, tm=1024, tn=512):
    t, d = x2.shape
    n = w_bf.shape[1]
    assert t % tm == 0 and n % tn == 0 and seq % tm == 0
    pos_blocks = seq // tm
    return pl.pallas_call(
        functools.partial(_ln_inproj_kernel, tn=tn),
        grid=(t // tm, n // tn),
        in_specs=[
            pl.BlockSpec((tm, d), lambda i, j: (i, 0)),
            pl.BlockSpec((1, d), lambda i, j: (0, 0)),
            pl.BlockSpec((1, d), lambda i, j: (0, 0)),
            pl.BlockSpec((d, tn), lambda i, j: (0, j)),
            pl.BlockSpec((1, HEAD_DIM), lambda i, j: (0, 0)),
            pl.BlockSpec((1, HEAD_DIM), lambda i, j: (0, 0)),
            pl.BlockSpec((tm, HEAD_DIM), lambda i, j: (i % pos_blocks, 0)),
            pl.BlockSpec((tm, HEAD_DIM), lambda i, j: (i % pos_blocks, 0)),
        ],
        out_specs=pl.BlockSpec((tm, tn), lambda i, j: (i, j)),
        out_shape=jax.ShapeDtypeStruct((t, n), BF16),
        scratch_shapes=[pltpu.VMEM((tm, d), BF16)],
        compiler_params=_params(("parallel", "arbitrary")),
        name="ln_inproj",
    )(x2, g, b, w_bf, qg, kg, cs, sn)


def _attn_kernel(q_ref, k_ref, v_ref, o_ref, *, tq, tk):
    seq = k_ref.shape[0]
    m_rows = Q_GROUP * tq
    q = jnp.concatenate([q_ref[:, h * HEAD_DIM:(h + 1) * HEAD_DIM] for h in range(Q_GROUP)], axis=0)

    def body(c, carry):
        m, l, acc = carry
        start = pl.multiple_of(c * tk, tk)
        kc = k_ref[pl.ds(start, tk), :]
        vc = v_ref[pl.ds(start, tk), :]
        s = lax.dot_general(q, kc, (((1,), (1,)), ((), ())), preferred_element_type=F32)
        m_new = jnp.maximum(m, jnp.max(s, axis=-1, keepdims=True))
        p = jnp.exp(s - m_new)
        alpha = jnp.exp(m - m_new)
        l_new = alpha * l + jnp.sum(p, axis=-1, keepdims=True)
        acc_new = alpha * acc + jnp.dot(p.astype(BF16), vc, preferred_element_type=F32)
        return m_new, l_new, acc_new

    m0 = jnp.full((m_rows, 1), -jnp.inf, F32)
    l0 = jnp.zeros((m_rows, 1), F32)
    a0 = jnp.zeros((m_rows, HEAD_DIM), F32)
    _, l, acc = lax.fori_loop(0, seq // tk, body, (m0, l0, a0))
    out = acc / l
    for h in range(Q_GROUP):
        o_ref[:, h * HEAD_DIM:(h + 1) * HEAD_DIM] = out[h * tq:(h + 1) * tq, :].astype(BF16)


def _attention(proj, batch, seq, tq=128, tk=512):
    t = proj.shape[0]
    nq = seq // tq
    gw = Q_GROUP * HEAD_DIM
    k_col0 = ATTN_WIDTH // HEAD_DIM
    v_col0 = (ATTN_WIDTH + KV_WIDTH) // HEAD_DIM
    return pl.pallas_call(
        functools.partial(_attn_kernel, tq=tq, tk=tk),
        grid=(batch, N_KV_HEADS, nq),
        in_specs=[
            pl.BlockSpec((tq, gw), lambda b, g, i: (b * nq + i, g)),
            pl.BlockSpec((seq, HEAD_DIM), lambda b, g, i: (b, k_col0 + g)),
            pl.BlockSpec((seq, HEAD_DIM), lambda b, g, i: (b, v_col0 + g)),
        ],
        out_specs=pl.BlockSpec((tq, gw), lambda b, g, i: (b * nq + i, g)),
        out_shape=jax.ShapeDtypeStruct((t, ATTN_WIDTH), BF16),
        compiler_params=_params(("parallel", "parallel", "arbitrary")),
        name="attention",
    )(proj, proj, proj)


def _hy_pre_kernel(x0_ref, x1_ref, hv_ref,
                   p0_ref, p1_ref, pv_ref, n0_ref, n1_ref, nv_ref,
                   w0_ref, w1_ref, wv_ref, b0_ref, b1_ref, bv_ref,
                   z_ref, x0c_ref):
    r = pl.program_id(2)
    last = pl.num_programs(2) - 1
    ts = x0_ref.shape[0]
    row = lax.broadcasted_iota(jnp.int32, x0_ref.shape, 0)

    def conv(x_ref, p_ref, n_ref, w_ref, b_ref):
        x = x_ref[...].astype(F32)
        prev_row = jnp.where(r == 0, 0.0, p_ref[7:8, :].astype(F32))
        next_row = jnp.where(r == last, 0.0, n_ref[0:1, :].astype(F32))
        up = jnp.where(row == 0, prev_row, pltpu.roll(x, 1, 0))
        dn = jnp.where(row == ts - 1, next_row, pltpu.roll(x, ts - 1, 0))
        w = w_ref[...]
        return b_ref[...] + up * w[0:1, :] + x * w[1:2, :] + dn * w[2:3, :]

    x0c = conv(x0_ref, p0_ref, n0_ref, w0_ref, b0_ref)
    x1c = conv(x1_ref, p1_ref, n1_ref, w1_ref, b1_ref)
    hvc = conv(hv_ref, pv_ref, nv_ref, wv_ref, bv_ref)
    z_ref[...] = (hvc * x1c).astype(BF16)
    x0c_ref[...] = x0c.astype(BF16)


def _hy_pre(proj, conv_w, conv_b, batch, seq, hw, col0, ts=512, tc=256):
    t = proj.shape[0]
    nr = seq // ts
    nct = hw // tc
    cb0 = col0 // tc
    halo = 8
    hb = ts // halo

    def main(off):
        return pl.BlockSpec((ts, tc), lambda b, c, r: (b * nr + r, cb0 + off * nct + c))

    def prev(off):
        return pl.BlockSpec((halo, tc), lambda b, c, r: (jnp.maximum((b * nr + r) * hb - 1, 0), cb0 + off * nct + c))

    def nxt(off):
        return pl.BlockSpec((halo, tc), lambda b, c, r: (jnp.minimum((b * nr + r + 1) * hb, t // halo - 1),
                                                       cb0 + off * nct + c))

    def wspec(off):
        return pl.BlockSpec((SHORT_CONV, tc), lambda b, c, r: (0, off * nct + c))

    def bspec(off):
        return pl.BlockSpec((1, tc), lambda b, c, r: (0, off * nct + c))

    out_spec = pl.BlockSpec((ts, tc), lambda b, c, r: (b * nr + r, c))
    return pl.pallas_call(
        _hy_pre_kernel,
        grid=(batch, nct, nr),
        in_specs=[main(0), main(1), main(2), prev(0), prev(1), prev(2), nxt(0), nxt(1), nxt(2),
                  wspec(0), wspec(1), wspec(2), bspec(0), bspec(1), bspec(2)],
        out_specs=[out_spec, out_spec],
        out_shape=[jax.ShapeDtypeStruct((t, hw), BF16), jax.ShapeDtypeStruct((t, hw), BF16)],
        compiler_params=_params(("parallel", "parallel", "parallel")),
        name="hy_pre",
    )(proj, proj, proj, proj, proj, proj, proj, proj, proj,
      conv_w, conv_w, conv_w, conv_b, conv_b, conv_b)


def _filter_kernel(feat_ref, w1_ref, b1_ref, f1_ref, w2_ref, b2_ref, f2_ref, w3_ref, dl_ref,
                   o_ref, ss_ref, *, seq):
    i = pl.program_id(0)
    tr = feat_ref.shape[0]
    hi = lax.Precision.HIGHEST
    feats = feat_ref[...]
    h = jnp.sin(f1_ref[...] * (jnp.dot(feats, w1_ref[...], precision=hi, preferred_element_type=F32) + b1_ref[...]))
    h = jnp.sin(f2_ref[...] * (jnp.dot(h, w2_ref[...], precision=hi, preferred_element_type=F32) + b2_ref[...]))
    filt = jnp.dot(h, w3_ref[...], precision=hi, preferred_element_type=F32)
    tpos = feats[:, 0:1]
    decay = jnp.exp(-tpos * jnp.abs(dl_ref[...]))
    row = i * tr + lax.broadcasted_iota(jnp.int32, filt.shape, 0)
    val = jnp.where(row == seq, 0.0, filt * decay)

    @pl.when(i == 0)
    def _():
        ss_ref[...] = jnp.zeros_like(ss_ref)

    ss_ref[...] += jnp.sum(val * val, axis=0, keepdims=True)
    o_ref[...] = val.astype(BF16)


def _filter_time(feats, w1p, b1, f1, w2, b2, f2, w3, deltas, seq, tr=1024):
    n2 = feats.shape[0]
    c = w3.shape[1] // 2
    hid = w2.shape[0]
    half_blocks = seq // tr
    const = lambda i: (0, 0)
    return pl.pallas_call(
        functools.partial(_filter_kernel, seq=seq),
        grid=(n2 // tr,),
        in_specs=[
            pl.BlockSpec((tr, LANES), lambda i: (i, 0)),
            pl.BlockSpec((LANES, hid), const),
            pl.BlockSpec((1, hid), const),
            pl.BlockSpec((1, hid), const),
            pl.BlockSpec((hid, hid), const),
            pl.BlockSpec((1, hid), const),
            pl.BlockSpec((1, hid), const),
            pl.BlockSpec((hid, c), lambda i: (0, (i >= half_blocks).astype(jnp.int32))),
            pl.BlockSpec((1, c), const),
        ],
        out_specs=[pl.BlockSpec((tr, c), lambda i: (i, 0)), pl.BlockSpec((1, c), const)],
        out_shape=[jax.ShapeDtypeStruct((n2, c), BF16), jax.ShapeDtypeStruct((1, c), F32)],
        compiler_params=_params(("arbitrary",)),
        name="filter_time",
    )(feats, w1p, b1, f1, w2, b2, f2, w3, deltas)


def _dft_a_kernel(x_ref, tab_ref, o_ref):
    x = x_ref[...]
    x = x.reshape(-1, x.shape[-1])
    o_ref[0] = jnp.dot(tab_ref[0], x, preferred_element_type=F32).astype(o_ref.dtype)


def _dft_stage_a(x3, tab, n_out_rows):
    g, r, wc = x3.shape
    c = wc // DFT_MINOR
    return pl.pallas_call(
        _dft_a_kernel,
        grid=(DFT_MINOR,),
        in_specs=[pl.BlockSpec((g, r, c), lambda t2: (0, 0, t2)),
                  pl.BlockSpec((1, n_out_rows, g * r), lambda t2: (t2, 0, 0))],
        out_specs=pl.BlockSpec((1, n_out_rows, c), lambda t2: (t2, 0, 0)),
        out_shape=jax.ShapeDtypeStruct((DFT_MINOR, n_out_rows, c), BF16),
        compiler_params=_params(("parallel",)),
        name="dft_stage_a",
    )(x3, tab)


def _filt_b_kernel(re_ref, im_ref, mf_ref, ss_ref, o_ref, *, n_total):
    rhs = jnp.concatenate([re_ref[...], im_ref[...]], axis=0)
    spec = jnp.dot(mf_ref[...], rhs, preferred_element_type=F32)
    scale = lax.rsqrt(ss_ref[...] + FILTER_EPS) * (1.0 / n_total)
    o_ref[0] = spec * scale


def _filter_stage_b(a_arr, mf, sumsq, n1):
    _, rows, c = a_arr.shape
    a2 = a_arr.reshape(DFT_MINOR, rows * c)
    n_total = n1 * DFT_MINOR
    return pl.pallas_call(
        functools.partial(_filt_b_kernel, n_total=n_total),
        grid=(n1,),
        in_specs=[pl.BlockSpec((DFT_MINOR, c), lambda k1: (0, k1)),
                  pl.BlockSpec((DFT_MINOR, c), lambda k1: (0, n1 + k1)),
                  pl.BlockSpec((2 * DFT_MINOR, 2 * DFT_MINOR), lambda k1: (0, 0)),
                  pl.BlockSpec((1, c), lambda k1: (0, 0))],
        out_specs=pl.BlockSpec((1, 2 * DFT_MINOR, c), lambda k1: (k1, 0, 0)),
        out_shape=jax.ShapeDtypeStruct((n1, 2 * DFT_MINOR, c), F32),
        compiler_params=_params(("parallel",)),
        name="filter_stage_b",
    )(a2, a2, mf, sumsq)


def _conv_b_kernel(re_ref, im_ref, mf_ref, mi_ref, h_ref, o_ref):
    half = DFT_MINOR
    rhs = jnp.concatenate([re_ref[...], im_ref[...]], axis=0)
    spec = jnp.dot(mf_ref[...], rhs, preferred_element_type=F32)
    hh = h_ref[0]
    xr, xi = spec[:half], spec[half:]
    hr, hi = hh[:half], hh[half:]
    yr = xr * hr - xi * hi
    yi = xr * hi + xi * hr
    y = jnp.concatenate([yr, yi], axis=0).astype(BF16)
    o_ref[0] = jnp.dot(mi_ref[...], y, preferred_element_type=F32).astype(o_ref.dtype)


def _conv_stage_b(a_arr, mf, mi, hspec, n1):
    _, rows, c = a_arr.shape
    a2 = a_arr.reshape(DFT_MINOR, rows * c)
    return pl.pallas_call(
        _conv_b_kernel,
        grid=(n1,),
        in_specs=[pl.BlockSpec((DFT_MINOR, c), lambda k1: (0, k1)),
                  pl.BlockSpec((DFT_MINOR, c), lambda k1: (0, n1 + k1)),
                  pl.BlockSpec((2 * DFT_MINOR, 2 * DFT_MINOR), lambda k1: (0, 0)),
                  pl.BlockSpec((2 * DFT_MINOR, 2 * DFT_MINOR), lambda k1: (0, 0)),
                  pl.BlockSpec((1, 2 * DFT_MINOR, c), lambda k1: (k1, 0, 0))],
        out_specs=pl.BlockSpec((1, 2 * DFT_MINOR, c), lambda k1: (k1, 0, 0)),
        out_shape=jax.ShapeDtypeStruct((n1, 2 * DFT_MINOR, c), BF16),
        compiler_params=_params(("parallel",)),
        name="conv_stage_b",
    )(a2, a2, mf, mi, hspec)


def _conv_out_kernel(re_ref, im_ref, tab_ref, z_ref, x0_ref, d_ref, o_ref):
    rhs = jnp.concatenate([re_ref[...], im_ref[...]], axis=0)
    y = jnp.dot(tab_ref[0], rhs, preferred_element_type=F32)
    y = y.reshape(z_ref.shape)
    z = z_ref[...].astype(F32)
    o_ref[...] = (x0_ref[...].astype(F32) * (y + z * d_ref[...])).astype(o_ref.dtype)


def _conv_stage_out(b_arr, tab, z3, x03, d_bias, n1):
    _, rows, c = b_arr.shape
    b2 = b_arr.reshape(n1, rows * c)
    g, r, _ = z3.shape
    blk = pl.BlockSpec((g, r, c), lambda t2: (0, 0, t2))
    return pl.pallas_call(
        _conv_out_kernel,
        grid=(DFT_MINOR,),
        in_specs=[pl.BlockSpec((n1, c), lambda t2: (0, t2)),
                  pl.BlockSpec((n1, c), lambda t2: (0, DFT_MINOR + t2)),
                  pl.BlockSpec((1, g * r, 2 * n1), lambda t2: (t2, 0, 0)),
                  blk, blk,
                  pl.BlockSpec((1, c), lambda t2: (0, 0))],
        out_specs=blk,
        out_shape=jax.ShapeDtypeStruct(z3.shape, BF16),
        compiler_params=_params(("parallel",)),
        name="conv_stage_out",
    )(b2, b2, tab, z3, x03, d_bias)


def _dft_tables(n1, n_sig_rows):
    n = n1 * DFT_MINOR
    k1 = jnp.arange(n1, dtype=jnp.int32)[None, :, None]
    t2 = jnp.arange(DFT_MINOR, dtype=jnp.int32)[:, None, None]

    def twiddled(n_t1):
        t1 = jnp.arange(n_t1, dtype=jnp.int32)[None, None, :]
        m = (k1 * (DFT_MINOR * t1 + t2)) % n
        ang = m.astype(F32) * (-2.0 * math.pi / n)
        return jnp.cos(ang), jnp.sin(ang)

    pr, pi = twiddled(n_sig_rows)
    tab_sig = jnp.concatenate([jnp.concatenate([pr, -pi], axis=2), jnp.concatenate([pi, pr], axis=2)], axis=1)
    prt, pit = jnp.swapaxes(pr, 1, 2), jnp.swapaxes(pi, 1, 2)
    tab_out = jnp.concatenate([jnp.concatenate([prt, pit], axis=2), jnp.concatenate([-pit, prt], axis=2)], axis=1)
    fr, fi = twiddled(n1)
    tab_filt = jnp.concatenate([fr, fi], axis=1)
    j = np.arange(DFT_MINOR)
    ang = -2.0 * np.pi * ((j[:, None] * j[None, :]) % DFT_MINOR) / DFT_MINOR
    cr, ci = np.cos(ang), np.sin(ang)
    mf = np.block([[cr, -ci], [ci, cr]])
    mi = np.block([[cr, ci], [-ci, cr]])
    return (tab_sig.astype(BF16), tab_out.astype(BF16), tab_filt.astype(BF16),
            jnp.asarray(mf, BF16), jnp.asarray(mi, BF16))


def _merge_a_kernel(a_ref, y_ref, ga_ref, gh_ref, bga_ref, bgh_ref, wa_ref, wh_ref, o_ref):
    ya = jnp.dot(a_ref[...], wa_ref[...], preferred_element_type=F32)
    yh = jnp.dot(y_ref[...], wh_ref[...], preferred_element_type=F32)
    ga = jax.nn.sigmoid(ga_ref[...].astype(F32) + bga_ref[...])
    gh = jax.nn.sigmoid(gh_ref[...].astype(F32) + bgh_ref[...])
    o_ref[...] = (ga * ya + gh * yh).astype(BF16)


def _merge_a(attn, yhy, proj, b_gate2, wa, wh, gate_col0, tm=512, tn=512):
    t, aw = attn.shape
    hw = yhy.shape[1]
    d = wa.shape[1]
    assert gate_col0 % tn == 0 and d % tn == 0
    gb = gate_col0 // tn
    nj = d // tn
    return pl.pallas_call(
        _merge_a_kernel,
        grid=(t // tm, nj),
        in_specs=[pl.BlockSpec((tm, aw), lambda i, j: (i, 0)),
                  pl.BlockSpec((tm, hw), lambda i, j: (i, 0)),
                  pl.BlockSpec((tm, tn), lambda i, j: (i, gb + j)),
                  pl.BlockSpec((tm, tn), lambda i, j: (i, gb + nj + j)),
                  pl.BlockSpec((1, tn), lambda i, j: (0, j)),
                  pl.BlockSpec((1, tn), lambda i, j: (0, nj + j)),
                  pl.BlockSpec((aw, tn), lambda i, j: (0, j)),
                  pl.BlockSpec((hw, tn), lambda i, j: (0, j))],
        out_specs=pl.BlockSpec((tm, tn), lambda i, j: (i, j)),
        out_shape=jax.ShapeDtypeStruct((t, d), BF16),
        compiler_params=_params(("parallel", "arbitrary")),
        name="merge_gated",
    )(attn, yhy, proj, proj, b_gate2, b_gate2, wa, wh)


def _merge_b_kernel(u_ref, x_ref, gi_ref, bi_ref, wo_ref, g1_ref, b1_ref, wr_ref, br_ref, h_ref, lg_ref):
    mixed = jnp.dot(u_ref[...], wo_ref[...], preferred_element_type=F32)
    h0 = _layer_norm(x_ref[...], gi_ref[...], bi_ref[...])
    h1 = _layer_norm(DEEPNORM_ALPHA * h0 + mixed, g1_ref[...], b1_ref[...])
    h_ref[...] = h1
    lg_ref[...] = jnp.dot(h1, wr_ref[...], precision=lax.Precision.HIGHEST,
                          preferred_element_type=F32) + br_ref[...]


def _merge_b(u, x2, gi, bi, wo, g1, b1, wr, br, tm=512):
    t, d = x2.shape
    const = lambda i: (0, 0)
    return pl.pallas_call(
        _merge_b_kernel,
        grid=(t // tm,),
        in_specs=[pl.BlockSpec((tm, d), lambda i: (i, 0)),
                  pl.BlockSpec((tm, d), lambda i: (i, 0)),
                  pl.BlockSpec((1, d), const), pl.BlockSpec((1, d), const),
                  pl.BlockSpec((d, d), const),
                  pl.BlockSpec((1, d), const), pl.BlockSpec((1, d), const),
                  pl.BlockSpec((d, LANES), const), pl.BlockSpec((1, LANES), const)],
        out_specs=[pl.BlockSpec((tm, d), lambda i: (i, 0)), pl.BlockSpec((tm, LANES), lambda i: (i, 0))],
        out_shape=[jax.ShapeDtypeStruct((t, d), F32), jax.ShapeDtypeStruct((t, LANES), F32)],
        compiler_params=_params(("parallel",)),
        name="merge_out_ln1",
    )(u, x2, gi, bi, wo, g1, b1, wr, br)


def _route_kernel(lg_ref, id_ref, gt_ref):
    x = lg_ref[...]
    lane = lax.broadcasted_iota(jnp.int32, x.shape, 1).astype(F32)
    big = jnp.float32(1 << 20)
    neg = -jnp.inf
    cmask = lane < N_GROUPS
    cm = jnp.max(jnp.where(cmask, x, neg), axis=-1, keepdims=True)
    grp = jnp.min(jnp.where(cmask & (x == cm), lane, big), axis=-1, keepdims=True)
    csum = jnp.sum(jnp.where(cmask, jnp.exp(x - cm), 0.0), axis=-1, keepdims=True)
    p_grp = 1.0 / csum
    lo = N_GROUPS + grp * EXPERTS_PER_GROUP
    fmask = (lane >= lo) & (lane < lo + EXPERTS_PER_GROUP)
    f1 = jnp.max(jnp.where(fmask, x, neg), axis=-1, keepdims=True)
    i1 = jnp.min(jnp.where(fmask & (x == f1), lane, big), axis=-1, keepdims=True)
    mask2 = fmask & (lane != i1)
    f2 = jnp.max(jnp.where(mask2, x, neg), axis=-1, keepdims=True)
    i2 = jnp.min(jnp.where(mask2 & (x == f2), lane, big), axis=-1, keepdims=True)
    e2 = jnp.exp(f2 - f1)
    g1 = p_grp / (1.0 + e2)
    g2 = p_grp * e2 / (1.0 + e2)
    ids = jnp.where(lane == 0, i1 - N_GROUPS, jnp.where(lane == 1, i2 - N_GROUPS, 0.0))
    id_ref[...] = ids.astype(jnp.int32)
    gt_ref[...] = jnp.where(lane == 0, g1, jnp.where(lane == 1, g2, 0.0))


def _route(logits, tm=2048):
    t = logits.shape[0]
    spec = pl.BlockSpec((tm, LANES), lambda i: (i, 0))
    return pl.pallas_call(
        _route_kernel,
        grid=(t // tm,),
        in_specs=[spec],
        out_specs=[spec, spec],
        out_shape=[jax.ShapeDtypeStruct((t, LANES), jnp.int32), jax.ShapeDtypeStruct((t, LANES), F32)],
        compiler_params=_params(("parallel",)),
        name="route_topk",
    )(logits)


def _row_gather(src_hbm, idx_ref, base, dst, sem, n_rows):
    def body(r, c):
        tok = idx_ref[base + r]
        pltpu.make_async_copy(src_hbm.at[pl.ds(tok, 1)], dst.at[pl.ds(r, 1)], sem).start()
        return c
    lax.fori_loop(0, n_rows, body, 0, unroll=8)


def _row_gather_wait(src_hbm, dst, sem, n_rows):
    def body(r, c):
        pltpu.make_async_copy(src_hbm.at[pl.ds(0, 1)], dst.at[pl.ds(r, 1)], sem).wait()
        return c
    lax.fori_loop(0, n_rows, body, 0, unroll=8)


def _moe_kernel(be_ref, nb_ref, tok_ref, h_hbm, wg_ref, wu_ref, wd_ref, o_ref, xbuf, sem):
    i = pl.program_id(0)
    n_used = nb_ref[0]
    slot = i % 2
    rows = xbuf.shape[1]

    @pl.when(i == 0)
    def _():
        _row_gather(h_hbm, tok_ref, 0, xbuf.at[0], sem.at[0], rows)

    @pl.when(i + 1 < n_used)
    def _():
        _row_gather(h_hbm, tok_ref, (i + 1) * rows, xbuf.at[1 - slot], sem.at[1 - slot], rows)

    @pl.when(i < n_used)
    def _():
        _row_gather_wait(h_hbm, xbuf.at[slot], sem.at[slot], rows)
        x = xbuf[slot].astype(BF16)
        g = jnp.dot(x, wg_ref[0], preferred_element_type=F32)
        u = jnp.dot(x, wu_ref[0], preferred_element_type=F32)
        hdn = (g * jax.nn.sigmoid(g) * u).astype(BF16)
        o_ref[...] = jnp.dot(hdn, wd_ref[0], preferred_element_type=F32)

    @pl.when(i >= n_used)
    def _():
        o_ref[...] = jnp.zeros_like(o_ref)


def _moe_ffn(blk_expert, n_used, row_tok, h1, wg, wu, wd, rows=MOE_ROWS):
    t, d = h1.shape
    f = wg.shape[2]
    n_blk = blk_expert.shape[0]
    grid_spec = pltpu.PrefetchScalarGridSpec(
        num_scalar_prefetch=3,
        grid=(n_blk,),
        in_specs=[pl.BlockSpec(memory_space=pl.ANY),
                  pl.BlockSpec((1, d, f), lambda i, be, nb, tk: (be[i], 0, 0)),
                  pl.BlockSpec((1, d, f), lambda i, be, nb, tk: (be[i], 0, 0)),
                  pl.BlockSpec((1, f, d), lambda i, be, nb, tk: (be[i], 0, 0))],
        out_specs=pl.BlockSpec((rows, d), lambda i, be, nb, tk: (i, 0)),
        scratch_shapes=[pltpu.VMEM((2, rows, d), F32), pltpu.SemaphoreType.DMA((2,))],
    )
    return pl.pallas_call(
        _moe_kernel,
        grid_spec=grid_spec,
        out_shape=jax.ShapeDtypeStruct((n_blk * rows, d), F32),
        compiler_params=_params(("arbitrary",)),
        name="moe_ffn",
    )(blk_expert, n_used, row_tok, h1, wg, wu, wd)


def _combine_kernel(p0_ref, p1_ref, ys_hbm, h_ref, gt_ref, g2_ref, b2_ref, o_ref, ybuf, sem):
    i = pl.program_id(0)
    n = pl.num_programs(0)
    slot = i % 2
    tm = h_ref.shape[0]

    def issue(step, s):
        _row_gather(ys_hbm, p0_ref, step * tm, ybuf.at[s, 0], sem.at[s], tm)
        _row_gather(ys_hbm, p1_ref, step * tm, ybuf.at[s, 1], sem.at[s], tm)

    @pl.when(i == 0)
    def _():
        issue(0, 0)

    @pl.when(i + 1 < n)
    def _():
        issue(i + 1, 1 - slot)

    _row_gather_wait(ys_hbm, ybuf.at[slot, 0], sem.at[slot], tm)
    _row_gather_wait(ys_hbm, ybuf.at[slot, 1], sem.at[slot], tm)
    gt = gt_ref[...]
    moe = gt[:, 0:1] * ybuf[slot, 0] + gt[:, 1:2] * ybuf[slot, 1]
    o_ref[...] = _layer_norm(DEEPNORM_ALPHA * h_ref[...] + moe, g2_ref[...], b2_ref[...])


def _combine(pos0, pos1, ys, h1, gates, g2, b2, tm=256):
    t, d = h1.shape
    const = lambda i, a, b: (0, 0)
    grid_spec = pltpu.PrefetchScalarGridSpec(
        num_scalar_prefetch=2,
        grid=(t // tm,),
        in_specs=[pl.BlockSpec(memory_space=pl.ANY),
                  pl.BlockSpec((tm, d), lambda i, a, b: (i, 0)),
                  pl.BlockSpec((tm, LANES), lambda i, a, b: (i, 0)),
                  pl.BlockSpec((1, d), const), pl.BlockSpec((1, d), const)],
        out_specs=pl.BlockSpec((tm, d), lambda i, a, b: (i, 0)),
        scratch_shapes=[pltpu.VMEM((2, 2, tm, d), F32), pltpu.SemaphoreType.DMA((2,))],
    )
    return pl.pallas_call(
        _combine_kernel,
        grid_spec=grid_spec,
        out_shape=jax.ShapeDtypeStruct((t, d), F32),
        compiler_params=_params(("arbitrary",)),
        name="moe_combine_ln2",
    )(pos0, pos1, ys, h1, gates, g2, b2)


def _rope_tables(seq):
    rows = seq // GRID_W
    row_idx = jnp.repeat(jnp.arange(rows, dtype=jnp.int32), GRID_W).astype(F32)
    col_idx = jnp.tile(jnp.arange(GRID_W, dtype=jnp.int32), rows).astype(F32)
    half = HEAD_DIM // 2
    inv_freq = ROPE_THETA ** (-jnp.arange(0, half, 2, dtype=F32) / half)
    ang_r = row_idx[:, None] * inv_freq[None, :]
    ang_c = col_idx[:, None] * inv_freq[None, :]
    cr, sr, cc, sc = jnp.cos(ang_r), jnp.sin(ang_r), jnp.cos(ang_c), jnp.sin(ang_c)
    cs = jnp.concatenate([cr, cr, cc, cc], axis=-1)
    sn = jnp.concatenate([-sr, sr, -sc, sc], axis=-1)
    return cs, sn


def _filter_features(seq):
    n = jnp.arange(2 * seq, dtype=jnp.int32)
    j = jnp.where(n <= seq, n, 2 * seq - n)
    j = jnp.where(n == seq, 0, j)
    t = (j.astype(F32) / (seq - 1))[:, None]
    w = (2.0 * math.pi * j.astype(F32) / seq)[:, None]
    bands = jnp.linspace(1e-4, FILTER_BANDS - 1, FILTER_BANDS, dtype=F32)[None, :]
    feats = jnp.concatenate([t, jnp.cos(bands * w), -jnp.sin(bands * w)], axis=-1)
    return jnp.pad(feats, ((0, 0), (0, LANES - FILTER_EMB)))


def _dispatch_plan(ids, n_tok, rows):
    e_flat = ids.reshape(-1)
    onehot = (e_flat[:, None] == jnp.arange(N_EXPERTS, dtype=jnp.int32)[None, :]).astype(jnp.int32)
    before = jnp.cumsum(onehot, axis=0) - onehot
    rank = jnp.sum(before * onehot, axis=1)
    counts = jnp.sum(onehot, axis=0)
    padded = (counts + rows - 1) // rows * rows
    pad_end = jnp.cumsum(padded)
    pad_start = pad_end - padded
    dest = pad_start[e_flat] + rank
    n_rows = n_tok * TOP_K + N_EXPERTS * rows
    n_blk = n_rows // rows
    tok_flat = jnp.arange(n_tok * TOP_K, dtype=jnp.int32) // TOP_K
    row_tok = jnp.zeros((n_rows,), jnp.int32).at[dest].set(tok_flat)
    blk_start = jnp.arange(n_blk, dtype=jnp.int32) * rows
    blk_expert = jnp.minimum(jnp.searchsorted(pad_end, blk_start, side="right"), N_EXPERTS - 1).astype(jnp.int32)
    n_used = (pad_end[-1] // rows).astype(jnp.int32).reshape(1)
    pos = dest.reshape(n_tok, TOP_K).astype(jnp.int32)
    return blk_expert, n_used, row_tok, pos[:, 0], pos[:, 1]


def kernel(x, ln_in_g, ln_in_b, w_in, b_gate, q_norm_g, k_norm_g, hy_conv_w, hy_conv_b, filt_w1, filt_b1, filt_f1, filt_w2, filt_b2, filt_f2, filt_w3, hy_bias_d, w_attn_o, w_hy_o, w_out, ln1_g, ln1_b, w_route_grp, b_route_grp, w_route_exp, b_route_exp, w_exp_gate, w_exp_up, w_exp_down, ln2_g, ln2_b):
    batch, seq, d = x.shape
    assert batch == 2, "the long convolution packs exactly two batch rows as one complex signal"
    t = batch * seq
    hw = hy_bias_d.shape[1]
    l = 0
    x2 = x.reshape(t, d)
    row = lambda v: v.reshape(1, -1)

    cs, sn = _rope_tables(seq)
    qg = row(q_norm_g[l]) * (HEAD_DIM ** -0.5)
    kg = row(k_norm_g[l])
    proj = _ln_inproj(x2, row(ln_in_g), row(ln_in_b), w_in[l].astype(BF16), qg, kg, cs, sn, seq)

    attn = _attention(proj, batch, seq)

    hy_col0 = ATTN_WIDTH + 2 * KV_WIDTH
    z, x0c = _hy_pre(proj, hy_conv_w[l], row(hy_conv_b[l]), batch, seq, hw, hy_col0)
    n1 = 2 * seq // DFT_MINOR
    tab_sig, tab_out, tab_filt, mf, mi = _dft_tables(n1, n1 // 2)
    feats = _filter_features(seq)
    w1p = jnp.pad(filt_w1[l], ((0, LANES - FILTER_EMB), (0, 0)))
    min_decay = math.log(DECAY_TARGET) / SLOW_DECAY_PCT
    max_decay = math.log(DECAY_TARGET) / FAST_DECAY_PCT
    deltas = jnp.linspace(min_decay, max_decay, hw, dtype=F32)[None, :]
    two_sided, sumsq = _filter_time(feats, w1p, row(filt_b1[l]), row(filt_f1[l]), filt_w2[l], row(filt_b2[l]),
                                    row(filt_f2[l]), filt_w3[l], deltas, seq)
    fa = _dft_stage_a(two_sided.reshape(1, n1, DFT_MINOR * hw), tab_filt, 2 * n1)
    hspec = _filter_stage_b(fa, mf, sumsq, n1)
    z3 = z.reshape(batch, n1 // 2, DFT_MINOR * hw)
    x03 = x0c.reshape(batch, n1 // 2, DFT_MINOR * hw)
    za = _dft_stage_a(z3, tab_sig, 2 * n1)
    zb = _conv_stage_b(za, mf, mi, hspec, n1)
    yhy = _conv_stage_out(zb, tab_out, z3, x03, row(hy_bias_d[l]), n1).reshape(t, hw)

    gate_col0 = hy_col0 + 3 * hw
    u = _merge_a(attn, yhy, proj, row(b_gate[l]), w_attn_o[l].astype(BF16), w_hy_o[l].astype(BF16), gate_col0)
    n_r = N_GROUPS + N_EXPERTS
    wr = jnp.pad(jnp.concatenate([w_route_grp[l], w_route_exp[l]], axis=1), ((0, 0), (0, LANES - n_r)))
    br = jnp.pad(jnp.concatenate([b_route_grp[l], b_route_exp[l]]), (0, LANES - n_r)).reshape(1, LANES)
    h1, logits = _merge_b(u, x2, row(ln_in_g), row(ln_in_b), w_out[l].astype(BF16), row(ln1_g[l]), row(ln1_b[l]),
                          wr, br)

    ids, gates = _route(logits)
    blk_expert, n_used, row_tok, pos0, pos1 = _dispatch_plan(ids[:, :TOP_K], t, MOE_ROWS)
    ys = _moe_ffn(blk_expert, n_used, row_tok, h1, w_exp_gate[l].astype(BF16), w_exp_up[l].astype(BF16),
                  w_exp_down[l].astype(BF16))
    out = _combine(pos0, pos1, ys, h1, gates, row(ln2_g[l]), row(ln2_b[l]))
    return out.reshape(batch, seq, d)
```

```python
import functools
import math

import jax
import jax.numpy as jnp
import numpy as np
from jax import lax
from jax.experimental import pallas as pl
from jax.experimental.pallas import tpu as pltpu

F32 = jnp.float32
BF16 = jnp.bfloat16

GRID_W = 64
N_Q_HEADS = 8
N_KV_HEADS = 2
HEAD_DIM = 128
Q_GROUP = N_Q_HEADS // N_KV_HEADS
ATTN_WIDTH = N_Q_HEADS * HEAD_DIM
KV_WIDTH = N_KV_HEADS * HEAD_DIM
ROPE_THETA = 10000.0
QK_EPS = 1e-6
SHORT_CONV = 3
FILTER_BANDS = 16
FILTER_EMB = 1 + 2 * FILTER_BANDS
DECAY_TARGET = 1e-2
FAST_DECAY_PCT = 0.3
SLOW_DECAY_PCT = 1.5
FILTER_EPS = 1e-6
N_GROUPS = 4
EXPERTS_PER_GROUP = 8
N_EXPERTS = N_GROUPS * EXPERTS_PER_GROUP
TOP_K = 2
LN_EPS = 1e-5
DEPTH = 1
DEEPNORM_ALPHA = (2 * DEPTH) ** 0.25

LANES = 128
V7X_VMEM_LIMIT = 56 * 1024 * 1024
DFT_MINOR = 128

MOE_ROWS = 256


def _params(sem, vmem=V7X_VMEM_LIMIT):
    return pltpu.CompilerParams(dimension_semantics=sem, vmem_limit_bytes=vmem)


def _layer_norm(x, g, b):
    mu = jnp.mean(x, axis=-1, keepdims=True)
    xc = x - mu
    var = jnp.mean(xc * xc, axis=-1, keepdims=True)
    return xc * lax.rsqrt(var + LN_EPS) * g + b


def _norm_rope(acc, gain, cs, sn):
    ms = jnp.mean(acc * acc, axis=-1, keepdims=True)
    xn = acc * lax.rsqrt(ms + QK_EPS) * gain
    return xn * cs + pltpu.roll(xn, HEAD_DIM // 2, 1) * sn


def _ln_inproj_kernel(x_ref, g_ref, b_ref, w_ref, qg_ref, kg_ref, cs_ref, sn_ref, o_ref, h_scr, *, tn):
    j = pl.program_id(1)
    nchunk = tn // HEAD_DIM
    n_q = ATTN_WIDTH // HEAD_DIM
    n_qk = n_q + N_KV_HEADS

    @pl.when(j == 0)
    def _():
        h = _layer_norm(x_ref[...], g_ref[...], b_ref[...])
        h_scr[...] = h.astype(BF16)

    acc = jnp.dot(h_scr[...], w_ref[...], preferred_element_type=F32)

    def store(first_plain_chunk_fn):
        for c in range(nchunk):
            sl = slice(c * HEAD_DIM, (c + 1) * HEAD_DIM)
            kind = first_plain_chunk_fn(c)
            if kind == "q":
                o_ref[:, sl] = _norm_rope(acc[:, sl], qg_ref[...], cs_ref[...], sn_ref[...]).astype(BF16)
            elif kind == "k":
                o_ref[:, sl] = _norm_rope(acc[:, sl], kg_ref[...], cs_ref[...], sn_ref[...]).astype(BF16)
            else:
                o_ref[:, sl] = acc[:, sl].astype(BF16)

    n_q_tiles = n_q // nchunk
    assert n_q % nchunk == 0 and N_KV_HEADS <= nchunk

    @pl.when(j < n_q_tiles)
    def _():
        store(lambda c: "q")

    @pl.when(j == n_q_tiles)
    def _():
        store(lambda c: "k" if c < N_KV_HEADS else "p")

    @pl.when(j > n_q_tiles)
    def _():
        store(lambda c: "p")


def _ln_inproj(x2, g, b, w_bf, qg, kg, cs, sn, seq, tm=1024, tn=512):
    t, d = x2.shape
    n = w_bf.shape[1]
    assert t % tm == 0 and n % tn == 0 and seq % tm == 0
    pos_blocks = seq // tm
    return pl.pallas_call(
        functools.partial(_ln_inproj_kernel, tn=tn),
        grid=(t // tm, n // tn),
        in_specs=[
            pl.BlockSpec((tm, d), lambda i, j: (i, 0)),
            pl.BlockSpec((1, d), lambda i, j: (0, 0)),
            pl.BlockSpec((1, d), lambda i, j: (0, 0)),
            pl.BlockSpec((d, tn), lambda i, j: (0, j)),
            pl.BlockSpec((1, HEAD_DIM), lambda i, j: (0, 0)),
            pl.BlockSpec((1, HEAD_DIM), lambda i, j: (0, 0)),
            pl.BlockSpec((tm, HEAD_DIM), lambda i, j: (i % pos_blocks, 0)),
            pl.BlockSpec((tm, HEAD_DIM), lambda i, j: (i % pos_blocks, 0)),
        ],
        out_specs=pl.BlockSpec((tm, tn), lambda i, j: (i, j)),
        out_shape=jax.ShapeDtypeStruct((t, n), BF16),
        scratch_shapes=[pltpu.VMEM((tm, d), BF16)],
        compiler_params=_params(("parallel", "arbitrary")),
        name="ln_inproj",
    )(x2, g, b, w_bf, qg, kg, cs, sn)


def _attn_kernel(q_ref, k_ref, v_ref, o_ref, vt_scr, *, tq, tk):
    seq = k_ref.shape[0]
    n_chunks = seq // tk
    m_cols = Q_GROUP * tq

    @pl.when(pl.program_id(2) == 0)
    def _():
        for c in range(n_chunks):
            vt_scr[c] = v_ref[c * tk:(c + 1) * tk, :].T

    qt = jnp.concatenate([q_ref[:, h * HEAD_DIM:(h + 1) * HEAD_DIM].T for h in range(Q_GROUP)], axis=1)

    def scores(c):
        start = pl.multiple_of(c * tk, tk)
        return jnp.dot(k_ref[pl.ds(start, tk), :], qt, preferred_element_type=F32)

    def update(c, s, m, l, acc):
        m_new = jnp.maximum(m, jnp.max(s, axis=0, keepdims=True))
        p = jnp.exp2(s - m_new)
        alpha = jnp.exp2(m - m_new)
        l_new = alpha * l + jnp.sum(p, axis=0, keepdims=True)
        acc_new = alpha * acc + jnp.dot(vt_scr[c], p.astype(BF16), preferred_element_type=F32)
        return m_new, l_new, acc_new

    def body(c, carry):
        m, l, acc, s = carry
        s_next = scores(jnp.minimum(c + 1, n_chunks - 1))
        m, l, acc = update(c, s, m, l, acc)
        return m, l, acc, s_next

    m0 = jnp.full((1, m_cols), -jnp.inf, F32)
    l0 = jnp.zeros((1, m_cols), F32)
    a0 = jnp.zeros((HEAD_DIM, m_cols), F32)
    _, l, acc, _ = lax.fori_loop(0, n_chunks, body, (m0, l0, a0, scores(0)), unroll=4)
    out = acc / l
    for h in range(Q_GROUP):
        o_ref[:, h * HEAD_DIM:(h + 1) * HEAD_DIM] = out[:, h * tq:(h + 1) * tq].T.astype(BF16)


def _attention(proj, batch, seq, tq=128, tk=512):
    t = proj.shape[0]
    nq = seq // tq
    gw = Q_GROUP * HEAD_DIM
    k_col0 = ATTN_WIDTH // HEAD_DIM
    v_col0 = (ATTN_WIDTH + KV_WIDTH) // HEAD_DIM
    return pl.pallas_call(
        functools.partial(_attn_kernel, tq=tq, tk=tk),
        grid=(batch, N_KV_HEADS, nq),
        in_specs=[
            pl.BlockSpec((tq, gw), lambda b, g, i: (b * nq + i, g)),
            pl.BlockSpec((seq, HEAD_DIM), lambda b, g, i: (b, k_col0 + g)),
            pl.BlockSpec((seq, HEAD_DIM), lambda b, g, i: (b, v_col0 + g)),
        ],
        out_specs=pl.BlockSpec((tq, gw), lambda b, g, i: (b * nq + i, g)),
        out_shape=jax.ShapeDtypeStruct((t, ATTN_WIDTH), BF16),
        scratch_shapes=[pltpu.VMEM((seq // tk, HEAD_DIM, tk), BF16)],
        compiler_params=_params(("parallel", "parallel", "arbitrary")),
        name="attention",
    )(proj, proj, proj)


def _hy_pre_kernel(x0_ref, x1_ref, hv_ref,
                   p0_ref, p1_ref, pv_ref, n0_ref, n1_ref, nv_ref,
                   w0_ref, w1_ref, wv_ref, b0_ref, b1_ref, bv_ref,
                   z_ref, x0c_ref):
    r = pl.program_id(2)
    last = pl.num_programs(2) - 1
    ts = x0_ref.shape[0]
    row = lax.broadcasted_iota(jnp.int32, x0_ref.shape, 0)

    def conv(x_ref, p_ref, n_ref, w_ref, b_ref):
        x = x_ref[...].astype(F32)
        prev_row = jnp.where(r == 0, 0.0, p_ref[7:8, :].astype(F32))
        next_row = jnp.where(r == last, 0.0, n_ref[0:1, :].astype(F32))
        up = jnp.where(row == 0, prev_row, pltpu.roll(x, 1, 0))
        dn = jnp.where(row == ts - 1, next_row, pltpu.roll(x, ts - 1, 0))
        w = w_ref[...]
        return b_ref[...] + up * w[0:1, :] + x * w[1:2, :] + dn * w[2:3, :]

    x0c = conv(x0_ref, p0_ref, n0_ref, w0_ref, b0_ref)
    x1c = conv(x1_ref, p1_ref, n1_ref, w1_ref, b1_ref)
    hvc = conv(hv_ref, pv_ref, nv_ref, wv_ref, bv_ref)
    z_ref[...] = (hvc * x1c).astype(BF16)
    x0c_ref[...] = x0c.astype(BF16)


def _hy_pre(proj, conv_w, conv_b, batch, seq, hw, col0, ts=512, tc=256):
    t = proj.shape[0]
    nr = seq // ts
    nct = hw // tc
    cb0 = col0 // tc
    halo = 8
    hb = ts // halo

    def main(off):
        return pl.BlockSpec((ts, tc), lambda b, c, r: (b * nr + r, cb0 + off * nct + c))

    def prev(off):
        return pl.BlockSpec((halo, tc), lambda b, c, r: (jnp.maximum((b * nr + r) * hb - 1, 0), cb0 + off * nct + c))

    def nxt(off):
        return pl.BlockSpec((halo, tc), lambda b, c, r: (jnp.minimum((b * nr + r + 1) * hb, t // halo - 1),
                                                       cb0 + off * nct + c))

    def wspec(off):
        return pl.BlockSpec((SHORT_CONV, tc), lambda b, c, r: (0, off * nct + c))

    def bspec(off):
        return pl.BlockSpec((1, tc), lambda b, c, r: (0, off * nct + c))

    out_spec = pl.BlockSpec((ts, tc), lambda b, c, r: (b * nr + r, c))
    return pl.pallas_call(
        _hy_pre_kernel,
        grid=(batch, nct, nr),
        in_specs=[main(0), main(1), main(2), prev(0), prev(1), prev(2), nxt(0), nxt(1), nxt(2),
                  wspec(0), wspec(1), wspec(2), bspec(0), bspec(1), bspec(2)],
        out_specs=[out_spec, out_spec],
        out_shape=[jax.ShapeDtypeStruct((t, hw), BF16), jax.ShapeDtypeStruct((t, hw), BF16)],
        compiler_params=_params(("parallel", "parallel", "parallel")),
        name="hy_pre",
    )(proj, proj, proj, proj, proj, proj, proj, proj, proj,
      conv_w, conv_w, conv_w, conv_b, conv_b, conv_b)


def _filter_kernel(feat_ref, w1_ref, b1_ref, f1_ref, w2_ref, b2_ref, f2_ref, w3_ref, dl_ref,
                   o_ref, ss_ref, *, seq):
    i = pl.program_id(0)
    tr = feat_ref.shape[0]
    hi = lax.Precision.HIGHEST
    feats = feat_ref[...]
    h = jnp.sin(f1_ref[...] * (jnp.dot(feats, w1_ref[...], precision=hi, preferred_element_type=F32) + b1_ref[...]))
    h = jnp.sin(f2_ref[...] * (jnp.dot(h, w2_ref[...], precision=hi, preferred_element_type=F32) + b2_ref[...]))
    filt = jnp.dot(h, w3_ref[...], precision=hi, preferred_element_type=F32)
    tpos = feats[:, 0:1]
    decay = jnp.exp(-tpos * jnp.abs(dl_ref[...]))
    row = i * tr + lax.broadcasted_iota(jnp.int32, filt.shape, 0)
    val = jnp.where(row == seq, 0.0, filt * decay)

    @pl.when(i == 0)
    def _():
        ss_ref[...] = jnp.zeros_like(ss_ref)

    ss_ref[...] += jnp.sum(val * val, axis=0, keepdims=True)
    o_ref[...] = val.astype(BF16)


def _filter_time(feats, w1p, b1, f1, w2, b2, f2, w3, deltas, seq, tr=1024):
    n2 = feats.shape[0]
    c = w3.shape[1] // 2
    hid = w2.shape[0]
    half_blocks = seq // tr
    const = lambda i: (0, 0)
    return pl.pallas_call(
        functools.partial(_filter_kernel, seq=seq),
        grid=(n2 // tr,),
        in_specs=[
            pl.BlockSpec((tr, LANES), lambda i: (i, 0)),
            pl.BlockSpec((LANES, hid), const),
            pl.BlockSpec((1, hid), const),
            pl.BlockSpec((1, hid), const),
            pl.BlockSpec((hid, hid), const),
            pl.BlockSpec((1, hid), const),
            pl.BlockSpec((1, hid), const),
            pl.BlockSpec((hid, c), lambda i: (0, (i >= half_blocks).astype(jnp.int32))),
            pl.BlockSpec((1, c), const),
        ],
        out_specs=[pl.BlockSpec((tr, c), lambda i: (i, 0)), pl.BlockSpec((1, c), const)],
        out_shape=[jax.ShapeDtypeStruct((n2, c), BF16), jax.ShapeDtypeStruct((1, c), F32)],
        compiler_params=_params(("arbitrary",)),
        name="filter_time",
    )(feats, w1p, b1, f1, w2, b2, f2, w3, deltas)


def _dft_a_kernel(x_ref, tab_ref, o_ref):
    x = x_ref[...]
    x = x.reshape(-1, x.shape[-1])
    o_ref[0] = jnp.dot(tab_ref[0], x, preferred_element_type=F32).astype(o_ref.dtype)


def _dft_stage_a(x3, tab, n_out_rows):
    g, r, wc = x3.shape
    c = wc // DFT_MINOR
    return pl.pallas_call(
        _dft_a_kernel,
        grid=(DFT_MINOR,),
        in_specs=[pl.BlockSpec((g, r, c), lambda t2: (0, 0, t2)),
                  pl.BlockSpec((1, n_out_rows, g * r), lambda t2: (t2, 0, 0))],
        out_specs=pl.BlockSpec((1, n_out_rows, c), lambda t2: (t2, 0, 0)),
        out_shape=jax.ShapeDtypeStruct((DFT_MINOR, n_out_rows, c), BF16),
        compiler_params=_params(("parallel",)),
        name="dft_stage_a",
    )(x3, tab)


def _filt_b_kernel(re_ref, im_ref, mf_ref, ss_ref, o_ref, *, n_total):
    rhs = jnp.concatenate([re_ref[...], im_ref[...]], axis=0)
    spec = jnp.dot(mf_ref[...], rhs, preferred_element_type=F32)
    scale = lax.rsqrt(ss_ref[...] + FILTER_EPS) * (1.0 / n_total)
    o_ref[0] = spec * scale


def _filter_stage_b(a_arr, mf, sumsq, n1):
    _, rows, c = a_arr.shape
    a2 = a_arr.reshape(DFT_MINOR, rows * c)
    n_total = n1 * DFT_MINOR
    return pl.pallas_call(
        functools.partial(_filt_b_kernel, n_total=n_total),
        grid=(n1,),
        in_specs=[pl.BlockSpec((DFT_MINOR, c), lambda k1: (0, k1)),
                  pl.BlockSpec((DFT_MINOR, c), lambda k1: (0, n1 + k1)),
                  pl.BlockSpec((2 * DFT_MINOR, 2 * DFT_MINOR), lambda k1: (0, 0)),
                  pl.BlockSpec((1, c), lambda k1: (0, 0))],
        out_specs=pl.BlockSpec((1, 2 * DFT_MINOR, c), lambda k1: (k1, 0, 0)),
        out_shape=jax.ShapeDtypeStruct((n1, 2 * DFT_MINOR, c), F32),
        compiler_params=_params(("parallel",)),
        name="filter_stage_b",
    )(a2, a2, mf, sumsq)


def _conv_b_kernel(re_ref, im_ref, mf_ref, mi_ref, h_ref, o_ref):
    half = DFT_MINOR
    rhs = jnp.concatenate([re_ref[...], im_ref[...]], axis=0)
    spec = jnp.dot(mf_ref[...], rhs, preferred_element_type=F32)
    hh = h_ref[0]
    xr, xi = spec[:half], spec[half:]
    hr, hi = hh[:half], hh[half:]
    yr = xr * hr - xi * hi
    yi = xr * hi + xi * hr
    y = jnp.concatenate([yr, yi], axis=0).astype(BF16)
    o_ref[0] = jnp.dot(mi_ref[...], y, preferred_element_type=F32).astype(o_ref.dtype)


def _conv_stage_b(a_arr, mf, mi, hspec, n1):
    _, rows, c = a_arr.shape
    a2 = a_arr.reshape(DFT_MINOR, rows * c)
    return pl.pallas_call(
        _conv_b_kernel,
        grid=(n1,),
        in_specs=[pl.BlockSpec((DFT_MINOR, c), lambda k1: (0, k1)),
                  pl.BlockSpec((DFT_MINOR, c), lambda k1: (0, n1 + k1)),
                  pl.BlockSpec((2 * DFT_MINOR, 2 * DFT_MINOR), lambda k1: (0, 0)),
                  pl.BlockSpec((2 * DFT_MINOR, 2 * DFT_MINOR), lambda k1: (0, 0)),
                  pl.BlockSpec((1, 2 * DFT_MINOR, c), lambda k1: (k1, 0, 0))],
        out_specs=pl.BlockSpec((1, 2 * DFT_MINOR, c), lambda k1: (k1, 0, 0)),
        out_shape=jax.ShapeDtypeStruct((n1, 2 * DFT_MINOR, c), BF16),
        compiler_params=_params(("parallel",)),
        name="conv_stage_b",
    )(a2, a2, mf, mi, hspec)


def _conv_out_kernel(re_ref, im_ref, tab_ref, z_ref, x0_ref, d_ref, o_ref):
    rhs = jnp.concatenate([re_ref[...], im_ref[...]], axis=0)
    y = jnp.dot(tab_ref[0], rhs, preferred_element_type=F32)
    y = y.reshape(z_ref.shape)
    z = z_ref[...].astype(F32)
    o_ref[...] = (x0_ref[...].astype(F32) * (y + z * d_ref[...])).astype(o_ref.dtype)


def _conv_stage_out(b_arr, tab, z3, x03, d_bias, n1):
    _, rows, c = b_arr.shape
    b2 = b_arr.reshape(n1, rows * c)
    g, r, _ = z3.shape
    blk = pl.BlockSpec((g, r, c), lambda t2: (0, 0, t2))
    return pl.pallas_call(
        _conv_out_kernel,
        grid=(DFT_MINOR,),
        in_specs=[pl.BlockSpec((n1, c), lambda t2: (0, t2)),
                  pl.BlockSpec((n1, c), lambda t2: (0, DFT_MINOR + t2)),
                  pl.BlockSpec((1, g * r, 2 * n1), lambda t2: (t2, 0, 0)),
                  blk, blk,
                  pl.BlockSpec((1, c), lambda t2: (0, 0))],
        out_specs=blk,
        out_shape=jax.ShapeDtypeStruct(z3.shape, BF16),
        compiler_params=_params(("parallel",)),
        name="conv_stage_out",
    )(b2, b2, tab, z3, x03, d_bias)


def _dft_tables(n1, n_sig_rows):
    n = n1 * DFT_MINOR
    k1 = jnp.arange(n1, dtype=jnp.int32)[None, :, None]
    t2 = jnp.arange(DFT_MINOR, dtype=jnp.int32)[:, None, None]

    def twiddled(n_t1):
        t1 = jnp.arange(n_t1, dtype=jnp.int32)[None, None, :]
        m = (k1 * (DFT_MINOR * t1 + t2)) % n
        ang = m.astype(F32) * (-2.0 * math.pi / n)
        return jnp.cos(ang), jnp.sin(ang)

    pr, pi = twiddled(n_sig_rows)
    tab_sig = jnp.concatenate([jnp.concatenate([pr, -pi], axis=2), jnp.concatenate([pi, pr], axis=2)], axis=1)
    prt, pit = jnp.swapaxes(pr, 1, 2), jnp.swapaxes(pi, 1, 2)
    tab_out = jnp.concatenate([jnp.concatenate([prt, pit], axis=2), jnp.concatenate([-pit, prt], axis=2)], axis=1)
    fr, fi = twiddled(n1)
    tab_filt = jnp.concatenate([fr, fi], axis=1)
    j = np.arange(DFT_MINOR)
    ang = -2.0 * np.pi * ((j[:, None] * j[None, :]) % DFT_MINOR) / DFT_MINOR
    cr, ci = np.cos(ang), np.sin(ang)
    mf = np.block([[cr, -ci], [ci, cr]])
    mi = np.block([[cr, ci], [-ci, cr]])
    return (tab_sig.astype(BF16), tab_out.astype(BF16), tab_filt.astype(BF16),
            jnp.asarray(mf, BF16), jnp.asarray(mi, BF16))


def _merge_a_kernel(a_ref, y_ref, ga_ref, gh_ref, bga_ref, bgh_ref, wa_ref, wh_ref, o_ref):
    ya = jnp.dot(a_ref[...], wa_ref[...], preferred_element_type=F32)
    yh = jnp.dot(y_ref[...], wh_ref[...], preferred_element_type=F32)
    ga = jax.nn.sigmoid(ga_ref[...].astype(F32) + bga_ref[...])
    gh = jax.nn.sigmoid(gh_ref[...].astype(F32) + bgh_ref[...])
    o_ref[...] = (ga * ya + gh * yh).astype(BF16)


def _merge_a(attn, yhy, proj, b_gate2, wa, wh, gate_col0, tm=512, tn=512):
    t, aw = attn.shape
    hw = yhy.shape[1]
    d = wa.shape[1]
    assert gate_col0 % tn == 0 and d % tn == 0
    gb = gate_col0 // tn
    nj = d // tn
    return pl.pallas_call(
        _merge_a_kernel,
        grid=(t // tm, nj),
        in_specs=[pl.BlockSpec((tm, aw), lambda i, j: (i, 0)),
                  pl.BlockSpec((tm, hw), lambda i, j: (i, 0)),
                  pl.BlockSpec((tm, tn), lambda i, j: (i, gb + j)),
                  pl.BlockSpec((tm, tn), lambda i, j: (i, gb + nj + j)),
                  pl.BlockSpec((1, tn), lambda i, j: (0, j)),
                  pl.BlockSpec((1, tn), lambda i, j: (0, nj + j)),
                  pl.BlockSpec((aw, tn), lambda i, j: (0, j)),
                  pl.BlockSpec((hw, tn), lambda i, j: (0, j))],
        out_specs=pl.BlockSpec((tm, tn), lambda i, j: (i, j)),
        out_shape=jax.ShapeDtypeStruct((t, d), BF16),
        compiler_params=_params(("parallel", "arbitrary")),
        name="merge_gated",
    )(attn, yhy, proj, proj, b_gate2, b_gate2, wa, wh)


def _merge_b_kernel(u_ref, x_ref, gi_ref, bi_ref, wo_ref, g1_ref, b1_ref, wr_ref, br_ref, h_ref, lg_ref):
    mixed = jnp.dot(u_ref[...], wo_ref[...], preferred_element_type=F32)
    h0 = _layer_norm(x_ref[...], gi_ref[...], bi_ref[...])
    h1 = _layer_norm(DEEPNORM_ALPHA * h0 + mixed, g1_ref[...], b1_ref[...])
    h_ref[...] = h1
    wr = wr_ref[...]
    h_hi = h1.astype(BF16)
    h_lo = (h1 - h_hi.astype(F32)).astype(BF16)
    w_hi = wr.astype(BF16)
    w_lo = (wr - w_hi.astype(F32)).astype(BF16)
    lg = jnp.dot(h_hi, w_hi, preferred_element_type=F32)
    lg += jnp.dot(h_hi, w_lo, preferred_element_type=F32)
    lg += jnp.dot(h_lo, w_hi, preferred_element_type=F32)
    lg_ref[...] = lg + br_ref[...]


def _merge_b(u, x2, gi, bi, wo, g1, b1, wr, br, tm=512):
    t, d = x2.shape
    const = lambda i: (0, 0)
    return pl.pallas_call(
        _merge_b_kernel,
        grid=(t // tm,),
        in_specs=[pl.BlockSpec((tm, d), lambda i: (i, 0)),
                  pl.BlockSpec((tm, d), lambda i: (i, 0)),
                  pl.BlockSpec((1, d), const), pl.BlockSpec((1, d), const),
                  pl.BlockSpec((d, d), const),
                  pl.BlockSpec((1, d), const), pl.BlockSpec((1, d), const),
                  pl.BlockSpec((d, LANES), const), pl.BlockSpec((1, LANES), const)],
        out_specs=[pl.BlockSpec((tm, d), lambda i: (i, 0)), pl.BlockSpec((tm, LANES), lambda i: (i, 0))],
        out_shape=[jax.ShapeDtypeStruct((t, d), F32), jax.ShapeDtypeStruct((t, LANES), F32)],
        compiler_params=_params(("parallel",)),
        name="merge_out_ln1",
    )(u, x2, gi, bi, wo, g1, b1, wr, br)


def _route_kernel(lg_ref, id_ref, gt_ref):
    x = lg_ref[...]
    lane = lax.broadcasted_iota(jnp.int32, x.shape, 1).astype(F32)
    big = jnp.float32(1 << 20)
    neg = -jnp.inf
    cmask = lane < N_GROUPS
    cm = jnp.max(jnp.where(cmask, x, neg), axis=-1, keepdims=True)
    grp = jnp.min(jnp.where(cmask & (x == cm), lane, big), axis=-1, keepdims=True)
    csum = jnp.sum(jnp.where(cmask, jnp.exp(x - cm), 0.0), axis=-1, keepdims=True)
    p_grp = 1.0 / csum
    lo = N_GROUPS + grp * EXPERTS_PER_GROUP
    fmask = (lane >= lo) & (lane < lo + EXPERTS_PER_GROUP)
    f1 = jnp.max(jnp.where(fmask, x, neg), axis=-1, keepdims=True)
    i1 = jnp.min(jnp.where(fmask & (x == f1), lane, big), axis=-1, keepdims=True)
    mask2 = fmask & (lane != i1)
    f2 = jnp.max(jnp.where(mask2, x, neg), axis=-1, keepdims=True)
    i2 = jnp.min(jnp.where(mask2 & (x == f2), lane, big), axis=-1, keepdims=True)
    e2 = jnp.exp(f2 - f1)
    g1 = p_grp / (1.0 + e2)
    g2 = p_grp * e2 / (1.0 + e2)
    ids = jnp.where(lane == 0, i1 - N_GROUPS, jnp.where(lane == 1, i2 - N_GROUPS, 0.0))
    id_ref[...] = ids.astype(jnp.int32)
    gt_ref[...] = jnp.where(lane == 0, g1, jnp.where(lane == 1, g2, 0.0))


def _route(logits, tm=2048):
    t = logits.shape[0]
    spec = pl.BlockSpec((tm, LANES), lambda i: (i, 0))
    return pl.pallas_call(
        _route_kernel,
        grid=(t // tm,),
        in_specs=[spec],
        out_specs=[spec, spec],
        out_shape=[jax.ShapeDtypeStruct((t, LANES), jnp.int32), jax.ShapeDtypeStruct((t, LANES), F32)],
        compiler_params=_params(("parallel",)),
        name="route_topk",
    )(logits)


def _row_gather(src_hbm, idx_ref, base, dst, sem, n_rows):
    def body(r, c):
        tok = idx_ref[base + r]
        pltpu.make_async_copy(src_hbm.at[pl.ds(tok, 1)], dst.at[pl.ds(r, 1)], sem).start()
        return c
    lax.fori_loop(0, n_rows, body, 0, unroll=8)


def _row_gather_wait(src_hbm, dst, sem, n_rows):
    def body(r, c):
        pltpu.make_async_copy(src_hbm.at[pl.ds(0, 1)], dst.at[pl.ds(r, 1)], sem).wait()
        return c
    lax.fori_loop(0, n_rows, body, 0, unroll=8)


def _moe_kernel(be_ref, nb_ref, tok_ref, h_hbm, wg_ref, wu_ref, wd_ref, o_ref, xbuf, sem):
    i = pl.program_id(0)
    n_used = nb_ref[0]
    slot = i % 2
    rows = xbuf.shape[1]

    @pl.when(i == 0)
    def _():
        _row_gather(h_hbm, tok_ref, 0, xbuf.at[0], sem.at[0], rows)

    @pl.when(i + 1 < n_used)
    def _():
        _row_gather(h_hbm, tok_ref, (i + 1) * rows, xbuf.at[1 - slot], sem.at[1 - slot], rows)

    @pl.when(i < n_used)
    def _():
        _row_gather_wait(h_hbm, xbuf.at[slot], sem.at[slot], rows)
        x = xbuf[slot].astype(BF16)
        g = jnp.dot(x, wg_ref[0], preferred_element_type=F32)
        u = jnp.dot(x, wu_ref[0], preferred_element_type=F32)
        hdn = (g * jax.nn.sigmoid(g) * u).astype(BF16)
        o_ref[...] = jnp.dot(hdn, wd_ref[0], preferred_element_type=F32)

    @pl.when(i >= n_used)
    def _():
        o_ref[...] = jnp.zeros_like(o_ref)


def _moe_ffn(blk_expert, n_used, row_tok, h1, wg, wu, wd, rows=MOE_ROWS):
    t, d = h1.shape
    f = wg.shape[2]
    n_blk = blk_expert.shape[0]
    grid_spec = pltpu.PrefetchScalarGridSpec(
        num_scalar_prefetch=3,
        grid=(n_blk,),
        in_specs=[pl.BlockSpec(memory_space=pl.ANY),
                  pl.BlockSpec((1, d, f), lambda i, be, nb, tk: (be[i], 0, 0)),
                  pl.BlockSpec((1, d, f), lambda i, be, nb, tk: (be[i], 0, 0)),
                  pl.BlockSpec((1, f, d), lambda i, be, nb, tk: (be[i], 0, 0))],
        out_specs=pl.BlockSpec((rows, d), lambda i, be, nb, tk: (i, 0)),
        scratch_shapes=[pltpu.VMEM((2, rows, d), F32), pltpu.SemaphoreType.DMA((2,))],
    )
    return pl.pallas_call(
        _moe_kernel,
        grid_spec=grid_spec,
        out_shape=jax.ShapeDtypeStruct((n_blk * rows, d), F32),
        compiler_params=_params(("arbitrary",)),
        name="moe_ffn",
    )(blk_expert, n_used, row_tok, h1, wg, wu, wd)


def _combine_kernel(p0_ref, p1_ref, ys_hbm, h_ref, gt_ref, g2_ref, b2_ref, o_ref, ybuf, sem):
    i = pl.program_id(0)
    n = pl.num_programs(0)
    slot = i % 2
    tm = h_ref.shape[0]

    def issue(step, s):
        _row_gather(ys_hbm, p0_ref, step * tm, ybuf.at[s, 0], sem.at[s], tm)
        _row_gather(ys_hbm, p1_ref, step * tm, ybuf.at[s, 1], sem.at[s], tm)

    @pl.when(i == 0)
    def _():
        issue(0, 0)

    @pl.when(i + 1 < n)
    def _():
        issue(i + 1, 1 - slot)

    _row_gather_wait(ys_hbm, ybuf.at[slot, 0], sem.at[slot], tm)
    _row_gather_wait(ys_hbm, ybuf.at[slot, 1], sem.at[slot], tm)
    gt = gt_ref[...]
    moe = gt[:, 0:1] * ybuf[slot, 0] + gt[:, 1:2] * ybuf[slot, 1]
    o_ref[...] = _layer_norm(DEEPNORM_ALPHA * h_ref[...] + moe, g2_ref[...], b2_ref[...])


def _combine(pos0, pos1, ys, h1, gates, g2, b2, tm=256):
    t, d = h1.shape
    const = lambda i, a, b: (0, 0)
    grid_spec = pltpu.PrefetchScalarGridSpec(
        num_scalar_prefetch=2,
        grid=(t // tm,),
        in_specs=[pl.BlockSpec(memory_space=pl.ANY),
                  pl.BlockSpec((tm, d), lambda i, a, b: (i, 0)),
                  pl.BlockSpec((tm, LANES), lambda i, a, b: (i, 0)),
                  pl.BlockSpec((1, d), const), pl.BlockSpec((1, d), const)],
        out_specs=pl.BlockSpec((tm, d), lambda i, a, b: (i, 0)),
        scratch_shapes=[pltpu.VMEM((2, 2, tm, d), F32), pltpu.SemaphoreType.DMA((2,))],
    )
    return pl.pallas_call(
        _combine_kernel,
        grid_spec=grid_spec,
        out_shape=jax.ShapeDtypeStruct((t, d), F32),
        compiler_params=_params(("arbitrary",)),
        name="moe_combine_ln2",
    )(pos0, pos1, ys, h1, gates, g2, b2)


def _rope_tables(seq):
    rows = seq // GRID_W
    row_idx = jnp.repeat(jnp.arange(rows, dtype=jnp.int32), GRID_W).astype(F32)
    col_idx = jnp.tile(jnp.arange(GRID_W, dtype=jnp.int32), rows).astype(F32)
    half = HEAD_DIM // 2
    inv_freq = ROPE_THETA ** (-jnp.arange(0, half, 2, dtype=F32) / half)
    ang_r = row_idx[:, None] * inv_freq[None, :]
    ang_c = col_idx[:, None] * inv_freq[None, :]
    cr, sr, cc, sc = jnp.cos(ang_r), jnp.sin(ang_r), jnp.cos(ang_c), jnp.sin(ang_c)
    cs = jnp.concatenate([cr, cc, cr, cc], axis=-1)
    sn = jnp.concatenate([-sr, -sc, sr, sc], axis=-1)
    return cs, sn


def _pair_split(v):
    lead = v.shape[:-1]
    q4 = v.reshape(lead + (-1, 2, 2, HEAD_DIM // 4))
    return jnp.swapaxes(q4, -3, -2).reshape(v.shape)


def _filter_features(seq):
    n = jnp.arange(2 * seq, dtype=jnp.int32)
    j = jnp.where(n <= seq, n, 2 * seq - n)
    j = jnp.where(n == seq, 0, j)
    t = (j.astype(F32) / (seq - 1))[:, None]
    w = (2.0 * math.pi * j.astype(F32) / seq)[:, None]
    bands = jnp.linspace(1e-4, FILTER_BANDS - 1, FILTER_BANDS, dtype=F32)[None, :]
    feats = jnp.concatenate([t, jnp.cos(bands * w), -jnp.sin(bands * w)], axis=-1)
    return jnp.pad(feats, ((0, 0), (0, LANES - FILTER_EMB)))


def _dispatch_plan(ids, n_tok, rows):
    e_flat = ids.reshape(-1)
    onehot = (e_flat[:, None] == jnp.arange(N_EXPERTS, dtype=jnp.int32)[None, :]).astype(jnp.int32)
    before = jnp.cumsum(onehot, axis=0) - onehot
    rank = jnp.sum(before * onehot, axis=1)
    counts = jnp.sum(onehot, axis=0)
    padded = (counts + rows - 1) // rows * rows
    pad_end = jnp.cumsum(padded)
    pad_start = pad_end - padded
    dest = pad_start[e_flat] + rank
    n_rows = n_tok * TOP_K + N_EXPERTS * rows
    n_blk = n_rows // rows
    tok_flat = jnp.arange(n_tok * TOP_K, dtype=jnp.int32) // TOP_K
    row_tok = jnp.zeros((n_rows,), jnp.int32).at[dest].set(tok_flat)
    blk_start = jnp.arange(n_blk, dtype=jnp.int32) * rows
    blk_expert = jnp.minimum(jnp.searchsorted(pad_end, blk_start, side="right"), N_EXPERTS - 1).astype(jnp.int32)
    n_used = (pad_end[-1] // rows).astype(jnp.int32).reshape(1)
    pos = dest.reshape(n_tok, TOP_K).astype(jnp.int32)
    return blk_expert, n_used, row_tok, pos[:, 0], pos[:, 1]


def kernel(x, ln_in_g, ln_in_b, w_in, b_gate, q_norm_g, k_norm_g, hy_conv_w, hy_conv_b, filt_w1, filt_b1, filt_f1, filt_w2, filt_b2, filt_f2, filt_w3, hy_bias_d, w_attn_o, w_hy_o, w_out, ln1_g, ln1_b, w_route_grp, b_route_grp, w_route_exp, b_route_exp, w_exp_gate, w_exp_up, w_exp_down, ln2_g, ln2_b):
    batch, seq, d = x.shape
    assert batch == 2, "the long convolution packs exactly two batch rows as one complex signal"
    t = batch * seq
    hw = hy_bias_d.shape[1]
    l = 0
    x2 = x.reshape(t, d)
    row = lambda v: v.reshape(1, -1)

    cs, sn = _rope_tables(seq)
    qg = row(q_norm_g[l]) * (HEAD_DIM ** -0.5 * math.log2(math.e))
    qg = _pair_split(qg)
    kg = _pair_split(row(k_norm_g[l]))
    n_qk = ATTN_WIDTH + KV_WIDTH
    w_bf = jnp.concatenate([_pair_split(w_in[l][:, :n_qk]).astype(BF16), w_in[l][:, n_qk:].astype(BF16)], axis=1)
    proj = _ln_inproj(x2, row(ln_in_g), row(ln_in_b), w_bf, qg, kg, cs, sn, seq)

    attn = _attention(proj, batch, seq)

    hy_col0 = ATTN_WIDTH + 2 * KV_WIDTH
    z, x0c = _hy_pre(proj, hy_conv_w[l], row(hy_conv_b[l]), batch, seq, hw, hy_col0)
    n1 = 2 * seq // DFT_MINOR
    tab_sig, tab_out, tab_filt, mf, mi = _dft_tables(n1, n1 // 2)
    feats = _filter_features(seq)
    w1p = jnp.pad(filt_w1[l], ((0, LANES - FILTER_EMB), (0, 0)))
    min_decay = math.log(DECAY_TARGET) / SLOW_DECAY_PCT
    max_decay = math.log(DECAY_TARGET) / FAST_DECAY_PCT
    deltas = jnp.linspace(min_decay, max_decay, hw, dtype=F32)[None, :]
    two_sided, sumsq = _filter_time(feats, w1p, row(filt_b1[l]), row(filt_f1[l]), filt_w2[l], row(filt_b2[l]),
                                    row(filt_f2[l]), filt_w3[l], deltas, seq)
    fa = _dft_stage_a(two_sided.reshape(1, n1, DFT_MINOR * hw), tab_filt, 2 * n1)
    hspec = _filter_stage_b(fa, mf, sumsq, n1)
    z3 = z.reshape(batch, n1 // 2, DFT_MINOR * hw)
    x03 = x0c.reshape(batch, n1 // 2, DFT_MINOR * hw)
    za = _dft_stage_a(z3, tab_sig, 2 * n1)
    zb = _conv_stage_b(za, mf, mi, hspec, n1)
    yhy = _conv_stage_out(zb, tab_out, z3, x03, row(hy_bias_d[l]), n1).reshape(t, hw)

    gate_col0 = hy_col0 + 3 * hw
    u = _merge_a(attn, yhy, proj, row(b_gate[l]), w_attn_o[l].astype(BF16), w_hy_o[l].astype(BF16), gate_col0)
    n_r = N_GROUPS + N_EXPERTS
    wr = jnp.pad(jnp.concatenate([w_route_grp[l], w_route_exp[l]], axis=1), ((0, 0), (0, LANES - n_r)))
    br = jnp.pad(jnp.concatenate([b_route_grp[l], b_route_exp[l]]), (0, LANES - n_r)).reshape(1, LANES)
    h1, logits = _merge_b(u, x2, row(ln_in_g), row(ln_in_b), w_out[l].astype(BF16), row(ln1_g[l]), row(ln1_b[l]),
                          wr, br)

    ids, gates = _route(logits)
    blk_expert, n_used, row_tok, pos0, pos1 = _dispatch_plan(ids[:, :TOP_K], t, MOE_ROWS)
    ys = _moe_ffn(blk_expert, n_used, row_tok, h1, w_exp_gate[l].astype(BF16), w_exp_up[l].astype(BF16),
                  w_exp_down[l].astype(BF16))
    out = _combine(pos0, pos1, ys, h1, gates, row(ln2_g[l]), row(ln2_b[l]))
    return out.reshape(batch, seq, d)
```

```python
import functools
import math

import jax
import jax.numpy as jnp
import numpy as np
from jax import lax
from jax.experimental import pallas as pl
from jax.experimental.pallas import tpu as pltpu

F32 = jnp.float32
BF16 = jnp.bfloat16

GRID_W = 64
N_Q_HEADS = 8
N_KV_HEADS = 2
HEAD_DIM = 128
Q_GROUP = N_Q_HEADS // N_KV_HEADS
ATTN_WIDTH = N_Q_HEADS * HEAD_DIM
KV_WIDTH = N_KV_HEADS * HEAD_DIM
ROPE_THETA = 10000.0
QK_EPS = 1e-6
SHORT_CONV = 3
FILTER_BANDS = 16
FILTER_EMB = 1 + 2 * FILTER_BANDS
DECAY_TARGET = 1e-2
FAST_DECAY_PCT = 0.3
SLOW_DECAY_PCT = 1.5
FILTER_EPS = 1e-6
N_GROUPS = 4
EXPERTS_PER_GROUP = 8
N_EXPERTS = N_GROUPS * EXPERTS_PER_GROUP
TOP_K = 2
LN_EPS = 1e-5
DEPTH = 1
DEEPNORM_ALPHA = (2 * DEPTH) ** 0.25

LANES = 128
V7X_VMEM_LIMIT = 56 * 1024 * 1024
DFT_MINOR = 128

MOE_ROWS = 256


def _params(sem, vmem=V7X_VMEM_LIMIT):
    return pltpu.CompilerParams(dimension_semantics=sem, vmem_limit_bytes=vmem)


def _layer_norm(x, g, b):
    mu = jnp.mean(x, axis=-1, keepdims=True)
    xc = x - mu
    var = jnp.mean(xc * xc, axis=-1, keepdims=True)
    return xc * lax.rsqrt(var + LN_EPS) * g + b


def _norm_rope(acc, gain, cs, sn):
    ms = jnp.mean(acc * acc, axis=-1, keepdims=True)
    xn = acc * lax.rsqrt(ms + QK_EPS) * gain
    return xn * cs + pltpu.roll(xn, HEAD_DIM // 2, 1) * sn


def _ln_inproj_kernel(x_ref, g_ref, b_ref, w_ref, qg_ref, kg_ref, cs_ref, sn_ref, o_ref, h_scr, *, tn):
    j = pl.program_id(1)
    nchunk = tn // HEAD_DIM
    n_q = ATTN_WIDTH // HEAD_DIM
    n_qk = n_q + N_KV_HEADS

    @pl.when(j == 0)
    def _():
        h = _layer_norm(x_ref[...], g_ref[...], b_ref[...])
        h_scr[...] = h.astype(BF16)

    acc = jnp.dot(h_scr[...], w_ref[...], preferred_element_type=F32)

    def store(first_plain_chunk_fn):
        for c in range(nchunk):
            sl = slice(c * HEAD_DIM, (c + 1) * HEAD_DIM)
            kind = first_plain_chunk_fn(c)
            if kind == "q":
                o_ref[:, sl] = _norm_rope(acc[:, sl], qg_ref[...], cs_ref[...], sn_ref[...]).astype(BF16)
            elif kind == "k":
                o_ref[:, sl] = _norm_rope(acc[:, sl], kg_ref[...], cs_ref[...], sn_ref[...]).astype(BF16)
            else:
                o_ref[:, sl] = acc[:, sl].astype(BF16)

    n_q_tiles = n_q // nchunk
    assert n_q % nchunk == 0 and N_KV_HEADS <= nchunk

    @pl.when(j < n_q_tiles)
    def _():
        store(lambda c: "q")

    @pl.when(j == n_q_tiles)
    def _():
        store(lambda c: "k" if c < N_KV_HEADS else "p")

    @pl.when(j > n_q_tiles)
    def _():
        store(lambda c: "p")


def _ln_inproj(x2, g, b, w_bf, qg, kg, cs, sn, seq, tm=1024, tn=512):
    t, d = x2.shape
    n = w_bf.shape[1]
    assert t % tm == 0 and n % tn == 0 and seq % tm == 0
    pos_blocks = seq // tm
    return pl.pallas_call(
        functools.partial(_ln_inproj_kernel, tn=tn),
        grid=(t // tm, n // tn),
        in_specs=[
            pl.BlockSpec((tm, d), lambda i, j: (i, 0)),
            pl.BlockSpec((1, d), lambda i, j: (0, 0)),
            pl.BlockSpec((1, d), lambda i, j: (0, 0)),
            pl.BlockSpec((d, tn), lambda i, j: (0, j)),
            pl.BlockSpec((1, HEAD_DIM), lambda i, j: (0, 0)),
            pl.BlockSpec((1, HEAD_DIM), lambda i, j: (0, 0)),
            pl.BlockSpec((tm, HEAD_DIM), lambda i, j: (i % pos_blocks, 0)),
            pl.BlockSpec((tm, HEAD_DIM), lambda i, j: (i % pos_blocks, 0)),
        ],
        out_specs=pl.BlockSpec((tm, tn), lambda i, j: (i, j)),
        out_shape=jax.ShapeDtypeStruct((t, n), BF16),
        scratch_shapes=[pltpu.VMEM((tm, d), BF16)],
        compiler_params=_params(("parallel", "arbitrary")),
        name="ln_inproj",
    )(x2, g, b, w_bf, qg, kg, cs, sn)


def _attn_kernel(q_ref, k_ref, v_ref, o_ref, vt_scr, *, tq, tk):
    seq = k_ref.shape[0]
    n_chunks = seq // tk
    m_cols = Q_GROUP * tq

    @pl.when(pl.program_id(2) == 0)
    def _():
        for c in range(n_chunks):
            vt_scr[c] = v_ref[c * tk:(c + 1) * tk, :].T

    qt = jnp.concatenate([q_ref[:, h * HEAD_DIM:(h + 1) * HEAD_DIM].T for h in range(Q_GROUP)], axis=1)

    def scores(c):
        start = pl.multiple_of(c * tk, tk)
        return jnp.dot(k_ref[pl.ds(start, tk), :], qt, preferred_element_type=F32)

    def update(c, s, m, l, acc):
        m_new = jnp.maximum(m, jnp.max(s, axis=0, keepdims=True))
        p = jnp.exp2(s - m_new)
        alpha = jnp.exp2(m - m_new)
        l_new = alpha * l + jnp.sum(p, axis=0, keepdims=True)
        acc_new = alpha * acc + jnp.dot(vt_scr[c], p.astype(BF16), preferred_element_type=F32)
        return m_new, l_new, acc_new

    def body(c, carry):
        m, l, acc, s = carry
        s_next = scores(jnp.minimum(c + 1, n_chunks - 1))
        m, l, acc = update(c, s, m, l, acc)
        return m, l, acc, s_next

    m0 = jnp.full((1, m_cols), -jnp.inf, F32)
    l0 = jnp.zeros((1, m_cols), F32)
    a0 = jnp.zeros((HEAD_DIM, m_cols), F32)
    _, l, acc, _ = lax.fori_loop(0, n_chunks, body, (m0, l0, a0, scores(0)), unroll=4)
    out = acc / l
    for h in range(Q_GROUP):
        o_ref[:, h * HEAD_DIM:(h + 1) * HEAD_DIM] = out[:, h * tq:(h + 1) * tq].T.astype(BF16)


def _attention(proj, batch, seq, tq=128, tk=512):
    t = proj.shape[0]
    nq = seq // tq
    gw = Q_GROUP * HEAD_DIM
    k_col0 = ATTN_WIDTH // HEAD_DIM
    v_col0 = (ATTN_WIDTH + KV_WIDTH) // HEAD_DIM
    return pl.pallas_call(
        functools.partial(_attn_kernel, tq=tq, tk=tk),
        grid=(batch, N_KV_HEADS, nq),
        in_specs=[
            pl.BlockSpec((tq, gw), lambda b, g, i: (b * nq + i, g)),
            pl.BlockSpec((seq, HEAD_DIM), lambda b, g, i: (b, k_col0 + g)),
            pl.BlockSpec((seq, HEAD_DIM), lambda b, g, i: (b, v_col0 + g)),
        ],
        out_specs=pl.BlockSpec((tq, gw), lambda b, g, i: (b * nq + i, g)),
        out_shape=jax.ShapeDtypeStruct((t, ATTN_WIDTH), BF16),
        scratch_shapes=[pltpu.VMEM((seq // tk, HEAD_DIM, tk), BF16)],
        compiler_params=_params(("parallel", "parallel", "arbitrary")),
        name="attention",
    )(proj, proj, proj)


def _hy_pre_kernel(x0_ref, x1_ref, hv_ref,
                   p0_ref, p1_ref, pv_ref, n0_ref, n1_ref, nv_ref,
                   w0_ref, w1_ref, wv_ref, b0_ref, b1_ref, bv_ref,
                   z_ref, x0c_ref):
    r = pl.program_id(2)
    last = pl.num_programs(2) - 1
    ts = x0_ref.shape[0]
    row = lax.broadcasted_iota(jnp.int32, x0_ref.shape, 0)

    def conv(x_ref, p_ref, n_ref, w_ref, b_ref):
        x = x_ref[...].astype(F32)
        prev_row = jnp.where(r == 0, 0.0, p_ref[7:8, :].astype(F32))
        next_row = jnp.where(r == last, 0.0, n_ref[0:1, :].astype(F32))
        up = jnp.where(row == 0, prev_row, pltpu.roll(x, 1, 0))
        dn = jnp.where(row == ts - 1, next_row, pltpu.roll(x, ts - 1, 0))
        w = w_ref[...]
        return b_ref[...] + up * w[0:1, :] + x * w[1:2, :] + dn * w[2:3, :]

    x0c = conv(x0_ref, p0_ref, n0_ref, w0_ref, b0_ref)
    x1c = conv(x1_ref, p1_ref, n1_ref, w1_ref, b1_ref)
    hvc = conv(hv_ref, pv_ref, nv_ref, wv_ref, bv_ref)
    z_ref[...] = (hvc * x1c).astype(BF16)
    x0c_ref[...] = x0c.astype(BF16)


def _hy_pre(proj, conv_w, conv_b, batch, seq, hw, col0, ts=512, tc=256):
    t = proj.shape[0]
    nr = seq // ts
    nct = hw // tc
    cb0 = col0 // tc
    halo = 8
    hb = ts // halo

    def main(off):
        return pl.BlockSpec((ts, tc), lambda b, c, r: (b * nr + r, cb0 + off * nct + c))

    def prev(off):
        return pl.BlockSpec((halo, tc), lambda b, c, r: (jnp.maximum((b * nr + r) * hb - 1, 0), cb0 + off * nct + c))

    def nxt(off):
        return pl.BlockSpec((halo, tc), lambda b, c, r: (jnp.minimum((b * nr + r + 1) * hb, t // halo - 1),
                                                       cb0 + off * nct + c))

    def wspec(off):
        return pl.BlockSpec((SHORT_CONV, tc), lambda b, c, r: (0, off * nct + c))

    def bspec(off):
        return pl.BlockSpec((1, tc), lambda b, c, r: (0, off * nct + c))

    out_spec = pl.BlockSpec((ts, tc), lambda b, c, r: (b * nr + r, c))
    return pl.pallas_call(
        _hy_pre_kernel,
        grid=(batch, nct, nr),
        in_specs=[main(0), main(1), main(2), prev(0), prev(1), prev(2), nxt(0), nxt(1), nxt(2),
                  wspec(0), wspec(1), wspec(2), bspec(0), bspec(1), bspec(2)],
        out_specs=[out_spec, out_spec],
        out_shape=[jax.ShapeDtypeStruct((t, hw), BF16), jax.ShapeDtypeStruct((t, hw), BF16)],
        compiler_params=_params(("parallel", "parallel", "parallel")),
        name="hy_pre",
    )(proj, proj, proj, proj, proj, proj, proj, proj, proj,
      conv_w, conv_w, conv_w, conv_b, conv_b, conv_b)


def _filter_kernel(feat_ref, w1_ref, b1_ref, f1_ref, w2_ref, b2_ref, f2_ref, w3_ref, dl_ref,
                   o_ref, ss_ref, *, seq):
    i = pl.program_id(0)
    tr = feat_ref.shape[0]
    hi = lax.Precision.HIGHEST
    feats = feat_ref[...]
    h = jnp.sin(f1_ref[...] * (jnp.dot(feats, w1_ref[...], precision=hi, preferred_element_type=F32) + b1_ref[...]))
    h = jnp.sin(f2_ref[...] * (jnp.dot(h, w2_ref[...], precision=hi, preferred_element_type=F32) + b2_ref[...]))
    filt = jnp.dot(h, w3_ref[...], precision=hi, preferred_element_type=F32)
    tpos = feats[:, 0:1]
    decay = jnp.exp(-tpos * jnp.abs(dl_ref[...]))
    row = i * tr + lax.broadcasted_iota(jnp.int32, filt.shape, 0)
    val = jnp.where(row == seq, 0.0, filt * decay)

    @pl.when(i == 0)
    def _():
        ss_ref[...] = jnp.zeros_like(ss_ref)

    ss_ref[...] += jnp.sum(val * val, axis=0, keepdims=True)
    o_ref[...] = val.astype(BF16)


def _filter_time(feats, w1p, b1, f1, w2, b2, f2, w3, deltas, seq, tr=1024):
    n2 = feats.shape[0]
    c = w3.shape[1] // 2
    hid = w2.shape[0]
    half_blocks = seq // tr
    const = lambda i: (0, 0)
    return pl.pallas_call(
        functools.partial(_filter_kernel, seq=seq),
        grid=(n2 // tr,),
        in_specs=[
            pl.BlockSpec((tr, LANES), lambda i: (i, 0)),
            pl.BlockSpec((LANES, hid), const),
            pl.BlockSpec((1, hid), const),
            pl.BlockSpec((1, hid), const),
            pl.BlockSpec((hid, hid), const),
            pl.BlockSpec((1, hid), const),
            pl.BlockSpec((1, hid), const),
            pl.BlockSpec((hid, c), lambda i: (0, (i >= half_blocks).astype(jnp.int32))),
            pl.BlockSpec((1, c), const),
        ],
        out_specs=[pl.BlockSpec((tr, c), lambda i: (i, 0)), pl.BlockSpec((1, c), const)],
        out_shape=[jax.ShapeDtypeStruct((n2, c), BF16), jax.ShapeDtypeStruct((1, c), F32)],
        compiler_params=_params(("arbitrary",)),
        name="filter_time",
    )(feats, w1p, b1, f1, w2, b2, f2, w3, deltas)


def _dft_a_kernel(x_ref, tab_ref, o_ref):
    x = x_ref[...]
    x = x.reshape(-1, x.shape[-1])
    o_ref[0] = jnp.dot(tab_ref[0], x, preferred_element_type=F32).astype(o_ref.dtype)


def _dft_stage_a(x3, tab, n_out_rows):
    g, r, wc = x3.shape
    c = wc // DFT_MINOR
    return pl.pallas_call(
        _dft_a_kernel,
        grid=(DFT_MINOR,),
        in_specs=[pl.BlockSpec((g, r, c), lambda t2: (0, 0, t2)),
                  pl.BlockSpec((1, n_out_rows, g * r), lambda t2: (t2, 0, 0))],
        out_specs=pl.BlockSpec((1, n_out_rows, c), lambda t2: (t2, 0, 0)),
        out_shape=jax.ShapeDtypeStruct((DFT_MINOR, n_out_rows, c), BF16),
        compiler_params=_params(("parallel",)),
        name="dft_stage_a",
    )(x3, tab)


def _filt_b_kernel(re_ref, im_ref, mf_ref, ss_ref, o_ref, *, n_total):
    rhs = jnp.concatenate([re_ref[...], im_ref[...]], axis=0)
    spec = jnp.dot(mf_ref[...], rhs, preferred_element_type=F32)
    scale = lax.rsqrt(ss_ref[...] + FILTER_EPS) * (1.0 / n_total)
    o_ref[0] = spec * scale


def _filter_stage_b(a_arr, mf, sumsq, n1):
    _, rows, c = a_arr.shape
    a2 = a_arr.reshape(DFT_MINOR, rows * c)
    n_total = n1 * DFT_MINOR
    return pl.pallas_call(
        functools.partial(_filt_b_kernel, n_total=n_total),
        grid=(n1,),
        in_specs=[pl.BlockSpec((DFT_MINOR, c), lambda k1: (0, k1)),
                  pl.BlockSpec((DFT_MINOR, c), lambda k1: (0, n1 + k1)),
                  pl.BlockSpec((2 * DFT_MINOR, 2 * DFT_MINOR), lambda k1: (0, 0)),
                  pl.BlockSpec((1, c), lambda k1: (0, 0))],
        out_specs=pl.BlockSpec((1, 2 * DFT_MINOR, c), lambda k1: (k1, 0, 0)),
        out_shape=jax.ShapeDtypeStruct((n1, 2 * DFT_MINOR, c), F32),
        compiler_params=_params(("parallel",)),
        name="filter_stage_b",
    )(a2, a2, mf, sumsq)


def _conv_b_kernel(re_ref, im_ref, mf_ref, mi_ref, h_ref, o_ref):
    half = DFT_MINOR
    rhs = jnp.concatenate([re_ref[...], im_ref[...]], axis=0)
    spec = jnp.dot(mf_ref[...], rhs, preferred_element_type=F32)
    hh = h_ref[0]
    xr, xi = spec[:half], spec[half:]
    hr, hi = hh[:half], hh[half:]
    yr = xr * hr - xi * hi
    yi = xr * hi + xi * hr
    y = jnp.concatenate([yr, yi], axis=0).astype(BF16)
    o_ref[0] = jnp.dot(mi_ref[...], y, preferred_element_type=F32).astype(o_ref.dtype)


def _conv_stage_b(a_arr, mf, mi, hspec, n1):
    _, rows, c = a_arr.shape
    a2 = a_arr.reshape(DFT_MINOR, rows * c)
    return pl.pallas_call(
        _conv_b_kernel,
        grid=(n1,),
        in_specs=[pl.BlockSpec((DFT_MINOR, c), lambda k1: (0, k1)),
                  pl.BlockSpec((DFT_MINOR, c), lambda k1: (0, n1 + k1)),
                  pl.BlockSpec((2 * DFT_MINOR, 2 * DFT_MINOR), lambda k1: (0, 0)),
                  pl.BlockSpec((2 * DFT_MINOR, 2 * DFT_MINOR), lambda k1: (0, 0)),
                  pl.BlockSpec((1, 2 * DFT_MINOR, c), lambda k1: (k1, 0, 0))],
        out_specs=pl.BlockSpec((1, 2 * DFT_MINOR, c), lambda k1: (k1, 0, 0)),
        out_shape=jax.ShapeDtypeStruct((n1, 2 * DFT_MINOR, c), BF16),
        compiler_params=_params(("parallel",)),
        name="conv_stage_b",
    )(a2, a2, mf, mi, hspec)


def _conv_out_kernel(re_ref, im_ref, tab_ref, z_ref, x0_ref, d_ref, o_ref):
    rhs = jnp.concatenate([re_ref[...], im_ref[...]], axis=0)
    y = jnp.dot(tab_ref[0], rhs, preferred_element_type=F32)
    y = y.reshape(z_ref.shape)
    z = z_ref[...].astype(F32)
    o_ref[...] = (x0_ref[...].astype(F32) * (y + z * d_ref[...])).astype(o_ref.dtype)


def _conv_stage_out(b_arr, tab, z3, x03, d_bias, n1):
    _, rows, c = b_arr.shape
    b2 = b_arr.reshape(n1, rows * c)
    g, r, _ = z3.shape
    blk = pl.BlockSpec((g, r, c), lambda t2: (0, 0, t2))
    return pl.pallas_call(
        _conv_out_kernel,
        grid=(DFT_MINOR,),
        in_specs=[pl.BlockSpec((n1, c), lambda t2: (0, t2)),
                  pl.BlockSpec((n1, c), lambda t2: (0, DFT_MINOR + t2)),
                  pl.BlockSpec((1, g * r, 2 * n1), lambda t2: (t2, 0, 0)),
                  blk, blk,
                  pl.BlockSpec((1, c), lambda t2: (0, 0))],
        out_specs=blk,
        out_shape=jax.ShapeDtypeStruct(z3.shape, BF16),
        compiler_params=_params(("parallel",)),
        name="conv_stage_out",
    )(b2, b2, tab, z3, x03, d_bias)


def _dft_tables(n1, n_sig_rows):
    n = n1 * DFT_MINOR
    k1 = jnp.arange(n1, dtype=jnp.int32)[None, :, None]
    t2 = jnp.arange(DFT_MINOR, dtype=jnp.int32)[:, None, None]

    def twiddled(n_t1):
        t1 = jnp.arange(n_t1, dtype=jnp.int32)[None, None, :]
        m = (k1 * (DFT_MINOR * t1 + t2)) % n
        ang = m.astype(F32) * (-2.0 * math.pi / n)
        return jnp.cos(ang), jnp.sin(ang)

    pr, pi = twiddled(n_sig_rows)
    tab_sig = jnp.concatenate([jnp.concatenate([pr, -pi], axis=2), jnp.concatenate([pi, pr], axis=2)], axis=1)
    prt, pit = jnp.swapaxes(pr, 1, 2), jnp.swapaxes(pi, 1, 2)
    tab_out = jnp.concatenate([jnp.concatenate([prt, pit], axis=2), jnp.concatenate([-pit, prt], axis=2)], axis=1)
    fr, fi = twiddled(n1)
    tab_filt = jnp.concatenate([fr, fi], axis=1)
    j = np.arange(DFT_MINOR)
    ang = -2.0 * np.pi * ((j[:, None] * j[None, :]) % DFT_MINOR) / DFT_MINOR
    cr, ci = np.cos(ang), np.sin(ang)
    mf = np.block([[cr, -ci], [ci, cr]])
    mi = np.block([[cr, ci], [-ci, cr]])
    return (tab_sig.astype(BF16), tab_out.astype(BF16), tab_filt.astype(BF16),
            jnp.asarray(mf, BF16), jnp.asarray(mi, BF16))


def _merge_a_kernel(a_ref, y_ref, ga_ref, gh_ref, bga_ref, bgh_ref, wa_ref, wh_ref, o_ref):
    ya = jnp.dot(a_ref[...], wa_ref[...], preferred_element_type=F32)
    yh = jnp.dot(y_ref[...], wh_ref[...], preferred_element_type=F32)
    ga = jax.nn.sigmoid(ga_ref[...].astype(F32) + bga_ref[...])
    gh = jax.nn.sigmoid(gh_ref[...].astype(F32) + bgh_ref[...])
    o_ref[...] = (ga * ya + gh * yh).astype(BF16)


def _merge_a(attn, yhy, proj, b_gate2, wa, wh, gate_col0, tm=512, tn=512):
    t, aw = attn.shape
    hw = yhy.shape[1]
    d = wa.shape[1]
    assert gate_col0 % tn == 0 and d % tn == 0
    gb = gate_col0 // tn
    nj = d // tn
    return pl.pallas_call(
        _merge_a_kernel,
        grid=(t // tm, nj),
        in_specs=[pl.BlockSpec((tm, aw), lambda i, j: (i, 0)),
                  pl.BlockSpec((tm, hw), lambda i, j: (i, 0)),
                  pl.BlockSpec((tm, tn), lambda i, j: (i, gb + j)),
                  pl.BlockSpec((tm, tn), lambda i, j: (i, gb + nj + j)),
                  pl.BlockSpec((1, tn), lambda i, j: (0, j)),
                  pl.BlockSpec((1, tn), lambda i, j: (0, nj + j)),
                  pl.BlockSpec((aw, tn), lambda i, j: (0, j)),
                  pl.BlockSpec((hw, tn), lambda i, j: (0, j))],
        out_specs=pl.BlockSpec((tm, tn), lambda i, j: (i, j)),
        out_shape=jax.ShapeDtypeStruct((t, d), BF16),
        compiler_params=_params(("parallel", "arbitrary")),
        name="merge_gated",
    )(attn, yhy, proj, proj, b_gate2, b_gate2, wa, wh)


def _merge_b_kernel(u_ref, x_ref, gi_ref, bi_ref, wo_ref, g1_ref, b1_ref, wr_ref, br_ref, h_ref, lg_ref):
    mixed = jnp.dot(u_ref[...], wo_ref[...], preferred_element_type=F32)
    h0 = _layer_norm(x_ref[...], gi_ref[...], bi_ref[...])
    h1 = _layer_norm(DEEPNORM_ALPHA * h0 + mixed, g1_ref[...], b1_ref[...])
    h_ref[...] = h1
    wr = wr_ref[...]
    h_hi = h1.astype(BF16)
    h_lo = (h1 - h_hi.astype(F32)).astype(BF16)
    w_hi = wr.astype(BF16)
    w_lo = (wr - w_hi.astype(F32)).astype(BF16)
    lg = jnp.dot(h_hi, w_hi, preferred_element_type=F32)
    lg += jnp.dot(h_hi, w_lo, preferred_element_type=F32)
    lg += jnp.dot(h_lo, w_hi, preferred_element_type=F32)
    lg_ref[...] = lg + br_ref[...]


def _merge_b(u, x2, gi, bi, wo, g1, b1, wr, br, tm=512):
    t, d = x2.shape
    const = lambda i: (0, 0)
    return pl.pallas_call(
        _merge_b_kernel,
        grid=(t // tm,),
        in_specs=[pl.BlockSpec((tm, d), lambda i: (i, 0)),
                  pl.BlockSpec((tm, d), lambda i: (i, 0)),
                  pl.BlockSpec((1, d), const), pl.BlockSpec((1, d), const),
                  pl.BlockSpec((d, d), const),
                  pl.BlockSpec((1, d), const), pl.BlockSpec((1, d), const),
                  pl.BlockSpec((d, LANES), const), pl.BlockSpec((1, LANES), const)],
        out_specs=[pl.BlockSpec((tm, d), lambda i: (i, 0)), pl.BlockSpec((tm, LANES), lambda i: (i, 0))],
        out_shape=[jax.ShapeDtypeStruct((t, d), F32), jax.ShapeDtypeStruct((t, LANES), F32)],
        compiler_params=_params(("parallel",)),
        name="merge_out_ln1",
    )(u, x2, gi, bi, wo, g1, b1, wr, br)


def _route_kernel(lg_ref, tri_ref, id_ref, gt_ref, cnt_ref, pre_scr):
    x = lg_ref[...]
    lane = lax.broadcasted_iota(jnp.int32, x.shape, 1).astype(F32)
    big = jnp.float32(1 << 20)
    neg = -jnp.inf
    cmask = lane < N_GROUPS
    cm = jnp.max(jnp.where(cmask, x, neg), axis=-1, keepdims=True)
    grp = jnp.min(jnp.where(cmask & (x == cm), lane, big), axis=-1, keepdims=True)
    csum = jnp.sum(jnp.where(cmask, jnp.exp(x - cm), 0.0), axis=-1, keepdims=True)
    p_grp = 1.0 / csum
    lo = N_GROUPS + grp * EXPERTS_PER_GROUP
    fmask = (lane >= lo) & (lane < lo + EXPERTS_PER_GROUP)
    f1 = jnp.max(jnp.where(fmask, x, neg), axis=-1, keepdims=True)
    i1 = jnp.min(jnp.where(fmask & (x == f1), lane, big), axis=-1, keepdims=True)
    mask2 = fmask & (lane != i1)
    f2 = jnp.max(jnp.where(mask2, x, neg), axis=-1, keepdims=True)
    i2 = jnp.min(jnp.where(mask2 & (x == f2), lane, big), axis=-1, keepdims=True)
    e2 = jnp.exp(f2 - f1)
    g1 = p_grp / (1.0 + e2)
    g2 = p_grp * e2 / (1.0 + e2)
    gt_ref[...] = jnp.where(lane == 0, g1, jnp.where(lane == 1, g2, 0.0))

    @pl.when(pl.program_id(0) == 0)
    def _():
        cnt_ref[...] = jnp.zeros_like(cnt_ref)

    sel1 = lane == i1
    sel2 = lane == i2
    chosen = jnp.where(sel1 | sel2, 1.0, 0.0)
    sub = tri_ref.shape[0]
    carry = cnt_ref[...]
    for s in range(x.shape[0] // sub):
        cs = chosen[s * sub:(s + 1) * sub]
        pre_scr[s * sub:(s + 1) * sub, :] = jnp.dot(tri_ref[...], cs.astype(BF16), preferred_element_type=F32) + carry
        carry = carry + jnp.sum(cs, axis=0, keepdims=True)
    cnt_ref[...] = carry
    before = pre_scr[...]
    r1 = jnp.sum(jnp.where(sel1, before, 0.0), axis=-1, keepdims=True)
    r2 = jnp.sum(jnp.where(sel2, before, 0.0), axis=-1, keepdims=True)
    ids = jnp.where(lane == 0, i1 - N_GROUPS, jnp.where(lane == 1, i2 - N_GROUPS,
                    jnp.where(lane == 2, r1, jnp.where(lane == 3, r2, 0.0))))
    id_ref[...] = ids.astype(jnp.int32)


def _route(logits, tm=2048, sub=256):
    t = logits.shape[0]
    spec = pl.BlockSpec((tm, LANES), lambda i: (i, 0))
    tri = jnp.asarray(np.tril(np.ones((sub, sub), np.float32), -1), BF16)
    return pl.pallas_call(
        _route_kernel,
        grid=(t // tm,),
        in_specs=[spec, pl.BlockSpec((sub, sub), lambda i: (0, 0))],
        out_specs=[spec, spec, pl.BlockSpec((1, LANES), lambda i: (0, 0))],
        out_shape=[jax.ShapeDtypeStruct((t, LANES), jnp.int32), jax.ShapeDtypeStruct((t, LANES), F32),
                   jax.ShapeDtypeStruct((1, LANES), F32)],
        scratch_shapes=[pltpu.VMEM((tm, LANES), F32)],
        compiler_params=_params(("arbitrary",)),
        name="route_topk",
    )(logits, tri)


def _invert_kernel(d0_ref, d1_ref, o_ref):
    n = o_ref.shape[0]
    n_tok = d0_ref.shape[0]

    def zero(i, c):
        o_ref[i] = 0
        return c
    lax.fori_loop(0, n, zero, 0, unroll=8)

    def put(t, c):
        o_ref[d0_ref[t]] = t
        o_ref[d1_ref[t]] = t
        return c
    lax.fori_loop(0, n_tok, put, 0, unroll=8)


def _invert(dest0, dest1, n_rows):
    return pl.pallas_call(
        _invert_kernel,
        grid_spec=pltpu.PrefetchScalarGridSpec(
            num_scalar_prefetch=2, grid=(1,), in_specs=[],
            out_specs=pl.BlockSpec(memory_space=pltpu.SMEM)),
        out_shape=jax.ShapeDtypeStruct((n_rows,), jnp.int32),
        compiler_params=_params(("arbitrary",)),
        name="dispatch_invert",
    )(dest0, dest1)


def _row_gather(src_hbm, idx_ref, base, dst, sem, n_rows):
    for r in range(n_rows):
        tok = idx_ref[base + r]
        pltpu.make_async_copy(src_hbm.at[pl.ds(tok, 1)], dst.at[pl.ds(r, 1)], sem).start()


def _row_gather_wait(src_hbm, dst, sem, n_rows):
    for r in range(n_rows):
        pltpu.make_async_copy(src_hbm.at[pl.ds(0, 1)], dst.at[pl.ds(r, 1)], sem).wait()


def _moe_kernel(be_ref, nb_ref, tok_ref, h_hbm, wg_ref, wu_ref, wd_ref, o_ref, xbuf, sem):
    i = pl.program_id(0)
    n_used = nb_ref[0]
    slot = i % 2
    rows = xbuf.shape[1]

    @pl.when(i == 0)
    def _():
        _row_gather(h_hbm, tok_ref, 0, xbuf.at[0], sem.at[0], rows)

    @pl.when(i < n_used)
    def _():
        _row_gather(h_hbm, tok_ref, (i + 1) * rows, xbuf.at[1 - slot], sem.at[1 - slot], rows)
        _row_gather_wait(h_hbm, xbuf.at[slot], sem.at[slot], rows)
        x = xbuf[slot].astype(BF16)
        g = jnp.dot(x, wg_ref[0], preferred_element_type=F32)
        u = jnp.dot(x, wu_ref[0], preferred_element_type=F32)
        hdn = (g * jax.nn.sigmoid(g) * u).astype(BF16)
        o_ref[...] = jnp.dot(hdn, wd_ref[0], preferred_element_type=F32)

    @pl.when(i == n_used)
    def _():
        _row_gather_wait(h_hbm, xbuf.at[slot], sem.at[slot], rows)

    @pl.when(i >= n_used)
    def _():
        o_ref[...] = jnp.zeros_like(o_ref)


def _moe_ffn(blk_expert, n_used, row_tok, h1, wg, wu, wd, rows=MOE_ROWS):
    t, d = h1.shape
    f = wg.shape[2]
    n_blk = blk_expert.shape[0]
    assert row_tok.shape[0] == (n_blk + 1) * rows
    grid_spec = pltpu.PrefetchScalarGridSpec(
        num_scalar_prefetch=3,
        grid=(n_blk,),
        in_specs=[pl.BlockSpec(memory_space=pl.ANY),
                  pl.BlockSpec((1, d, f), lambda i, be, nb, tk: (be[i], 0, 0)),
                  pl.BlockSpec((1, d, f), lambda i, be, nb, tk: (be[i], 0, 0)),
                  pl.BlockSpec((1, f, d), lambda i, be, nb, tk: (be[i], 0, 0))],
        out_specs=pl.BlockSpec((rows, d), lambda i, be, nb, tk: (i, 0)),
        scratch_shapes=[pltpu.VMEM((2, rows, d), F32), pltpu.SemaphoreType.DMA((2,))],
    )
    return pl.pallas_call(
        _moe_kernel,
        grid_spec=grid_spec,
        out_shape=jax.ShapeDtypeStruct((n_blk * rows, d), F32),
        compiler_params=_params(("arbitrary",)),
        name="moe_ffn",
    )(blk_expert, n_used, row_tok, h1, wg, wu, wd)


def _combine_kernel(p0_ref, p1_ref, ys_hbm, h_ref, gt_ref, g2_ref, b2_ref, o_ref, ybuf, sem):
    i = pl.program_id(0)
    n = pl.num_programs(0)
    slot = i % 2
    tm = h_ref.shape[0]

    def issue(step, s):
        _row_gather(ys_hbm, p0_ref, step * tm, ybuf.at[s, 0], sem.at[s], tm)
        _row_gather(ys_hbm, p1_ref, step * tm, ybuf.at[s, 1], sem.at[s], tm)

    def drain(s):
        _row_gather_wait(ys_hbm, ybuf.at[s, 0], sem.at[s], tm)
        _row_gather_wait(ys_hbm, ybuf.at[s, 1], sem.at[s], tm)

    @pl.when(i == 0)
    def _():
        issue(0, 0)

    issue(i + 1, 1 - slot)
    drain(slot)
    gt = gt_ref[...]
    moe = gt[:, 0:1] * ybuf[slot, 0] + gt[:, 1:2] * ybuf[slot, 1]
    o_ref[...] = _layer_norm(DEEPNORM_ALPHA * h_ref[...] + moe, g2_ref[...], b2_ref[...])

    @pl.when(i == n - 1)
    def _():
        drain(1 - slot)


def _combine(pos0, pos1, ys, h1, gates, g2, b2, tm=256):
    t, d = h1.shape
    assert pos0.shape[0] == t + tm and pos1.shape[0] == t + tm
    const = lambda i, a, b: (0, 0)
    grid_spec = pltpu.PrefetchScalarGridSpec(
        num_scalar_prefetch=2,
        grid=(t // tm,),
        in_specs=[pl.BlockSpec(memory_space=pl.ANY),
                  pl.BlockSpec((tm, d), lambda i, a, b: (i, 0)),
                  pl.BlockSpec((tm, LANES), lambda i, a, b: (i, 0)),
                  pl.BlockSpec((1, d), const), pl.BlockSpec((1, d), const)],
        out_specs=pl.BlockSpec((tm, d), lambda i, a, b: (i, 0)),
        scratch_shapes=[pltpu.VMEM((2, 2, tm, d), F32), pltpu.SemaphoreType.DMA((2,))],
    )
    return pl.pallas_call(
        _combine_kernel,
        grid_spec=grid_spec,
        out_shape=jax.ShapeDtypeStruct((t, d), F32),
        compiler_params=_params(("arbitrary",)),
        name="moe_combine_ln2",
    )(pos0, pos1, ys, h1, gates, g2, b2)


def _rope_tables(seq):
    rows = seq // GRID_W
    row_idx = jnp.repeat(jnp.arange(rows, dtype=jnp.int32), GRID_W).astype(F32)
    col_idx = jnp.tile(jnp.arange(GRID_W, dtype=jnp.int32), rows).astype(F32)
    half = HEAD_DIM // 2
    inv_freq = ROPE_THETA ** (-jnp.arange(0, half, 2, dtype=F32) / half)
    ang_r = row_idx[:, None] * inv_freq[None, :]
    ang_c = col_idx[:, None] * inv_freq[None, :]
    cr, sr, cc, sc = jnp.cos(ang_r), jnp.sin(ang_r), jnp.cos(ang_c), jnp.sin(ang_c)
    cs = jnp.concatenate([cr, cc, cr, cc], axis=-1)
    sn = jnp.concatenate([-sr, -sc, sr, sc], axis=-1)
    return cs, sn


def _pair_split(v):
    lead = v.shape[:-1]
    q4 = v.reshape(lead + (-1, 2, 2, HEAD_DIM // 4))
    return jnp.swapaxes(q4, -3, -2).reshape(v.shape)


def _filter_features(seq):
    n = jnp.arange(2 * seq, dtype=jnp.int32)
    j = jnp.where(n <= seq, n, 2 * seq - n)
    j = jnp.where(n == seq, 0, j)
    t = (j.astype(F32) / (seq - 1))[:, None]
    w = (2.0 * math.pi * j.astype(F32) / seq)[:, None]
    bands = jnp.linspace(1e-4, FILTER_BANDS - 1, FILTER_BANDS, dtype=F32)[None, :]
    feats = jnp.concatenate([t, jnp.cos(bands * w), -jnp.sin(bands * w)], axis=-1)
    return jnp.pad(feats, ((0, 0), (0, LANES - FILTER_EMB)))


def _dispatch_plan(ids, counts, n_tok, rows, tail):
    counts = counts.astype(jnp.int32)
    padded = (counts + rows - 1) // rows * rows
    pad_end = jnp.cumsum(padded)
    pad_start = pad_end - padded
    n_blk = (n_tok * TOP_K + N_EXPERTS * rows) // rows + 1
    blk_start = jnp.arange(n_blk, dtype=jnp.int32) * rows
    blk_expert = jnp.minimum(jnp.sum((pad_end[None, :] <= blk_start[:, None]).astype(jnp.int32), axis=1),
                             N_EXPERTS - 1)
    n_used = (pad_end[-1] // rows).astype(jnp.int32).reshape(1)
    experts = jnp.arange(N_EXPERTS, dtype=jnp.int32)[None, :]

    def dest(e, rank):
        start = jnp.sum(jnp.where(e[:, None] == experts, pad_start[None, :], 0), axis=1)
        return jnp.pad(start + rank, (0, tail))

    pos0 = dest(ids[:, 0], ids[:, 2])
    pos1 = dest(ids[:, 1], ids[:, 3])
    row_tok = _invert(pos0[:n_tok], pos1[:n_tok], (n_blk + 1) * rows)
    return blk_expert, n_used, row_tok, pos0, pos1


def kernel(x, ln_in_g, ln_in_b, w_in, b_gate, q_norm_g, k_norm_g, hy_conv_w, hy_conv_b, filt_w1, filt_b1, filt_f1, filt_w2, filt_b2, filt_f2, filt_w3, hy_bias_d, w_attn_o, w_hy_o, w_out, ln1_g, ln1_b, w_route_grp, b_route_grp, w_route_exp, b_route_exp, w_exp_gate, w_exp_up, w_exp_down, ln2_g, ln2_b):
    batch, seq, d = x.shape
    assert batch == 2, "the long convolution packs exactly two batch rows as one complex signal"
    t = batch * seq
    hw = hy_bias_d.shape[1]
    l = 0
    x2 = x.reshape(t, d)
    row = lambda v: v.reshape(1, -1)

    cs, sn = _rope_tables(seq)
    qg = row(q_norm_g[l]) * (HEAD_DIM ** -0.5 * math.log2(math.e))
    qg = _pair_split(qg)
    kg = _pair_split(row(k_norm_g[l]))
    n_qk = ATTN_WIDTH + KV_WIDTH
    w_bf = jnp.concatenate([_pair_split(w_in[l][:, :n_qk]).astype(BF16), w_in[l][:, n_qk:].astype(BF16)], axis=1)
    proj = _ln_inproj(x2, row(ln_in_g), row(ln_in_b), w_bf, qg, kg, cs, sn, seq)

    attn = _attention(proj, batch, seq)

    hy_col0 = ATTN_WIDTH + 2 * KV_WIDTH
    z, x0c = _hy_pre(proj, hy_conv_w[l], row(hy_conv_b[l]), batch, seq, hw, hy_col0)
    n1 = 2 * seq // DFT_MINOR
    tab_sig, tab_out, tab_filt, mf, mi = _dft_tables(n1, n1 // 2)
    feats = _filter_features(seq)
    w1p = jnp.pad(filt_w1[l], ((0, LANES - FILTER_EMB), (0, 0)))
    min_decay = math.log(DECAY_TARGET) / SLOW_DECAY_PCT
    max_decay = math.log(DECAY_TARGET) / FAST_DECAY_PCT
    deltas = jnp.linspace(min_decay, max_decay, hw, dtype=F32)[None, :]
    two_sided, sumsq = _filter_time(feats, w1p, row(filt_b1[l]), row(filt_f1[l]), filt_w2[l], row(filt_b2[l]),
                                    row(filt_f2[l]), filt_w3[l], deltas, seq)
    fa = _dft_stage_a(two_sided.reshape(1, n1, DFT_MINOR * hw), tab_filt, 2 * n1)
    hspec = _filter_stage_b(fa, mf, sumsq, n1)
    z3 = z.reshape(batch, n1 // 2, DFT_MINOR * hw)
    x03 = x0c.reshape(batch, n1 // 2, DFT_MINOR * hw)
    za = _dft_stage_a(z3, tab_sig, 2 * n1)
    zb = _conv_stage_b(za, mf, mi, hspec, n1)
    yhy = _conv_stage_out(zb, tab_out, z3, x03, row(hy_bias_d[l]), n1).reshape(t, hw)

    gate_col0 = hy_col0 + 3 * hw
    u = _merge_a(attn, yhy, proj, row(b_gate[l]), w_attn_o[l].astype(BF16), w_hy_o[l].astype(BF16), gate_col0)
    n_r = N_GROUPS + N_EXPERTS
    wr = jnp.pad(jnp.concatenate([w_route_grp[l], w_route_exp[l]], axis=1), ((0, 0), (0, LANES - n_r)))
    br = jnp.pad(jnp.concatenate([b_route_grp[l], b_route_exp[l]]), (0, LANES - n_r)).reshape(1, LANES)
    h1, logits = _merge_b(u, x2, row(ln_in_g), row(ln_in_b), w_out[l].astype(BF16), row(ln1_g[l]), row(ln1_b[l]),
                          wr, br)

    ids, gates, cnt = _route(logits)
    counts = cnt[0, N_GROUPS:N_GROUPS + N_EXPERTS]
    combine_tile = 256
    blk_expert, n_used, row_tok, pos0, pos1 = _dispatch_plan(ids, counts, t, MOE_ROWS, combine_tile)
    ys = _moe_ffn(blk_expert, n_used, row_tok, h1, w_exp_gate[l].astype(BF16), w_exp_up[l].astype(BF16),
                  w_exp_down[l].astype(BF16))
    out = _combine(pos0, pos1, ys, h1, gates, row(ln2_g[l]), row(ln2_b[l]), tm=combine_tile)
    return out.reshape(batch, seq, d)
```

```python
import functools
import math

import jax
import jax.numpy as jnp
import numpy as np
from jax import lax
from jax.experimental import pallas as pl
from jax.experimental.pallas import tpu as pltpu

F32 = jnp.float32
BF16 = jnp.bfloat16

GRID_W = 64
N_Q_HEADS = 8
N_KV_HEADS = 2
HEAD_DIM = 128
Q_GROUP = N_Q_HEADS // N_KV_HEADS
ATTN_WIDTH = N_Q_HEADS * HEAD_DIM
KV_WIDTH = N_KV_HEADS * HEAD_DIM
ROPE_THETA = 10000.0
QK_EPS = 1e-6
SHORT_CONV = 3
FILTER_BANDS = 16
FILTER_EMB = 1 + 2 * FILTER_BANDS
DECAY_TARGET = 1e-2
FAST_DECAY_PCT = 0.3
SLOW_DECAY_PCT = 1.5
FILTER_EPS = 1e-6
N_GROUPS = 4
EXPERTS_PER_GROUP = 8
N_EXPERTS = N_GROUPS * EXPERTS_PER_GROUP
TOP_K = 2
LN_EPS = 1e-5
DEPTH = 1
DEEPNORM_ALPHA = (2 * DEPTH) ** 0.25

LANES = 128
V7X_VMEM_LIMIT = 56 * 1024 * 1024
DFT_MINOR = 128

MOE_ROWS = 256


def _params(sem, vmem=V7X_VMEM_LIMIT):
    return pltpu.CompilerParams(dimension_semantics=sem, vmem_limit_bytes=vmem)


def _layer_norm(x, g, b):
    mu = jnp.mean(x, axis=-1, keepdims=True)
    xc = x - mu
    var = jnp.mean(xc * xc, axis=-1, keepdims=True)
    return xc * lax.rsqrt(var + LN_EPS) * g + b


def _norm_rope(acc, gain, cs, sn):
    ms = jnp.mean(acc * acc, axis=-1, keepdims=True)
    xn = acc * lax.rsqrt(ms + QK_EPS) * gain
    return xn * cs + pltpu.roll(xn, HEAD_DIM // 2, 1) * sn


def _ln_inproj_kernel(x_ref, g_ref, b_ref, w_ref, qg_ref, kg_ref, cs_ref, sn_ref, o_ref, h_scr, *, tn):
    j = pl.program_id(1)
    nchunk = tn // HEAD_DIM
    n_q = ATTN_WIDTH // HEAD_DIM
    n_qk = n_q + N_KV_HEADS

    @pl.when(j == 0)
    def _():
        h = _layer_norm(x_ref[...], g_ref[...], b_ref[...])
        h_scr[...] = h.astype(BF16)

    acc = jnp.dot(h_scr[...], w_ref[...], preferred_element_type=F32)

    def store(first_plain_chunk_fn):
        for c in range(nchunk):
            sl = slice(c * HEAD_DIM, (c + 1) * HEAD_DIM)
            kind = first_plain_chunk_fn(c)
            if kind == "q":
                o_ref[:, sl] = _norm_rope(acc[:, sl], qg_ref[...], cs_ref[...], sn_ref[...]).astype(BF16)
            elif kind == "k":
                o_ref[:, sl] = _norm_rope(acc[:, sl], kg_ref[...], cs_ref[...], sn_ref[...]).astype(BF16)
            else:
                o_ref[:, sl] = acc[:, sl].astype(BF16)

    n_q_tiles = n_q // nchunk
    assert n_q % nchunk == 0 and N_KV_HEADS <= nchunk

    @pl.when(j < n_q_tiles)
    def _():
        store(lambda c: "q")

    @pl.when(j == n_q_tiles)
    def _():
        store(lambda c: "k" if c < N_KV_HEADS else "p")

    @pl.when(j > n_q_tiles)
    def _():
        store(lambda c: "p")


def _ln_inproj(x2, g, b, w_bf, qg, kg, cs, sn, seq, tm=1024, tn=512):
    t, d = x2.shape
    n = w_bf.shape[1]
    assert t % tm == 0 and n % tn == 0 and seq % tm == 0
    pos_blocks = seq // tm
    return pl.pallas_call(
        functools.partial(_ln_inproj_kernel, tn=tn),
        grid=(t // tm, n // tn),
        in_specs=[
            pl.BlockSpec((tm, d), lambda i, j: (i, 0)),
            pl.BlockSpec((1, d), lambda i, j: (0, 0)),
            pl.BlockSpec((1, d), lambda i, j: (0, 0)),
            pl.BlockSpec((d, tn), lambda i, j: (0, j)),
            pl.BlockSpec((1, HEAD_DIM), lambda i, j: (0, 0)),
            pl.BlockSpec((1, HEAD_DIM), lambda i, j: (0, 0)),
            pl.BlockSpec((tm, HEAD_DIM), lambda i, j: (i % pos_blocks, 0)),
            pl.BlockSpec((tm, HEAD_DIM), lambda i, j: (i % pos_blocks, 0)),
        ],
        out_specs=pl.BlockSpec((tm, tn), lambda i, j: (i, j)),
        out_shape=jax.ShapeDtypeStruct((t, n), BF16),
        scratch_shapes=[pltpu.VMEM((tm, d), BF16)],
        compiler_params=_params(("parallel", "arbitrary")),
        name="ln_inproj",
    )(x2, g, b, w_bf, qg, kg, cs, sn)


def _attn_kernel(q_ref, k_ref, v_ref, o_ref, vt_scr, *, tq, tk):
    seq = k_ref.shape[0]
    n_chunks = seq // tk
    m_cols = Q_GROUP * tq

    @pl.when(pl.program_id(2) == 0)
    def _():
        for c in range(n_chunks):
            vt_scr[c] = v_ref[c * tk:(c + 1) * tk, :].T

    qt = jnp.concatenate([q_ref[:, h * HEAD_DIM:(h + 1) * HEAD_DIM].T for h in range(Q_GROUP)], axis=1)

    def scores(c):
        start = pl.multiple_of(c * tk, tk)
        return jnp.dot(k_ref[pl.ds(start, tk), :], qt, preferred_element_type=F32)

    def update(c, s, m, l, acc):
        m_new = jnp.maximum(m, jnp.max(s, axis=0, keepdims=True))
        p = jnp.exp2(s - m_new)
        alpha = jnp.exp2(m - m_new)
        l_new = alpha * l + jnp.sum(p, axis=0, keepdims=True)
        acc_new = alpha * acc + jnp.dot(vt_scr[c], p.astype(BF16), preferred_element_type=F32)
        return m_new, l_new, acc_new

    def body(c, carry):
        m, l, acc, s = carry
        s_next = scores(jnp.minimum(c + 1, n_chunks - 1))
        m, l, acc = update(c, s, m, l, acc)
        return m, l, acc, s_next

    m0 = jnp.full((1, m_cols), -jnp.inf, F32)
    l0 = jnp.zeros((1, m_cols), F32)
    a0 = jnp.zeros((HEAD_DIM, m_cols), F32)
    _, l, acc, _ = lax.fori_loop(0, n_chunks, body, (m0, l0, a0, scores(0)), unroll=4)
    out = acc / l
    for h in range(Q_GROUP):
        o_ref[:, h * HEAD_DIM:(h + 1) * HEAD_DIM] = out[:, h * tq:(h + 1) * tq].T.astype(BF16)


def _attention(proj, batch, seq, tq=128, tk=512):
    t = proj.shape[0]
    nq = seq // tq
    gw = Q_GROUP * HEAD_DIM
    k_col0 = ATTN_WIDTH // HEAD_DIM
    v_col0 = (ATTN_WIDTH + KV_WIDTH) // HEAD_DIM
    return pl.pallas_call(
        functools.partial(_attn_kernel, tq=tq, tk=tk),
        grid=(batch, N_KV_HEADS, nq),
        in_specs=[
            pl.BlockSpec((tq, gw), lambda b, g, i: (b * nq + i, g)),
            pl.BlockSpec((seq, HEAD_DIM), lambda b, g, i: (b, k_col0 + g)),
            pl.BlockSpec((seq, HEAD_DIM), lambda b, g, i: (b, v_col0 + g)),
        ],
        out_specs=pl.BlockSpec((tq, gw), lambda b, g, i: (b * nq + i, g)),
        out_shape=jax.ShapeDtypeStruct((t, ATTN_WIDTH), BF16),
        scratch_shapes=[pltpu.VMEM((seq // tk, HEAD_DIM, tk), BF16)],
        compiler_params=_params(("parallel", "parallel", "arbitrary")),
        name="attention",
    )(proj, proj, proj)


def _hy_pre_kernel(x0_ref, x1_ref, hv_ref,
                   p0_ref, p1_ref, pv_ref, n0_ref, n1_ref, nv_ref,
                   w0_ref, w1_ref, wv_ref, b0_ref, b1_ref, bv_ref,
                   z_ref, x0c_ref):
    r = pl.program_id(2)
    last = pl.num_programs(2) - 1
    ts = x0_ref.shape[0]
    row = lax.broadcasted_iota(jnp.int32, x0_ref.shape, 0)

    def conv(x_ref, p_ref, n_ref, w_ref, b_ref):
        x = x_ref[...].astype(F32)
        prev_row = jnp.where(r == 0, 0.0, p_ref[7:8, :].astype(F32))
        next_row = jnp.where(r == last, 0.0, n_ref[0:1, :].astype(F32))
        up = jnp.where(row == 0, prev_row, pltpu.roll(x, 1, 0))
        dn = jnp.where(row == ts - 1, next_row, pltpu.roll(x, ts - 1, 0))
        w = w_ref[...]
        return b_ref[...] + up * w[0:1, :] + x * w[1:2, :] + dn * w[2:3, :]

    x0c = conv(x0_ref, p0_ref, n0_ref, w0_ref, b0_ref)
    x1c = conv(x1_ref, p1_ref, n1_ref, w1_ref, b1_ref)
    hvc = conv(hv_ref, pv_ref, nv_ref, wv_ref, bv_ref)
    z_ref[...] = (hvc * x1c).astype(BF16)
    x0c_ref[...] = x0c.astype(BF16)


def _hy_pre(proj, conv_w, conv_b, batch, seq, hw, col0, ts=512, tc=256):
    t = proj.shape[0]
    nr = seq // ts
    nct = hw // tc
    cb0 = col0 // tc
    halo = 8
    hb = ts // halo

    def main(off):
        return pl.BlockSpec((ts, tc), lambda b, c, r: (b * nr + r, cb0 + off * nct + c))

    def prev(off):
        return pl.BlockSpec((halo, tc), lambda b, c, r: (jnp.maximum((b * nr + r) * hb - 1, 0), cb0 + off * nct + c))

    def nxt(off):
        return pl.BlockSpec((halo, tc), lambda b, c, r: (jnp.minimum((b * nr + r + 1) * hb, t // halo - 1),
                                                       cb0 + off * nct + c))

    def wspec(off):
        return pl.BlockSpec((SHORT_CONV, tc), lambda b, c, r: (0, off * nct + c))

    def bspec(off):
        return pl.BlockSpec((1, tc), lambda b, c, r: (0, off * nct + c))

    out_spec = pl.BlockSpec((ts, tc), lambda b, c, r: (b * nr + r, c))
    return pl.pallas_call(
        _hy_pre_kernel,
        grid=(batch, nct, nr),
        in_specs=[main(0), main(1), main(2), prev(0), prev(1), prev(2), nxt(0), nxt(1), nxt(2),
                  wspec(0), wspec(1), wspec(2), bspec(0), bspec(1), bspec(2)],
        out_specs=[out_spec, out_spec],
        out_shape=[jax.ShapeDtypeStruct((t, hw), BF16), jax.ShapeDtypeStruct((t, hw), BF16)],
        compiler_params=_params(("parallel", "parallel", "parallel")),
        name="hy_pre",
    )(proj, proj, proj, proj, proj, proj, proj, proj, proj,
      conv_w, conv_w, conv_w, conv_b, conv_b, conv_b)


def _dot3(a, b):
    a_hi = a.astype(BF16)
    a_lo = (a - a_hi.astype(F32)).astype(BF16)
    b_hi = b.astype(BF16)
    b_lo = (b - b_hi.astype(F32)).astype(BF16)
    out = jnp.dot(a_hi, b_hi, preferred_element_type=F32)
    out += jnp.dot(a_hi, b_lo, preferred_element_type=F32)
    out += jnp.dot(a_lo, b_hi, preferred_element_type=F32)
    return out


def _filter_kernel(feat_ref, w1_ref, b1_ref, f1_ref, w2_ref, b2_ref, f2_ref, w3_ref, dl_ref, o_ref, ss_ref):
    feats = feat_ref[...]
    feats = feats.reshape(-1, feats.shape[-1])
    h = jnp.sin(f1_ref[...] * (_dot3(feats, w1_ref[...]) + b1_ref[...]))
    h = jnp.sin(f2_ref[...] * (_dot3(h, w2_ref[...]) + b2_ref[...]))
    filt = jnp.dot(h.astype(BF16), w3_ref[...].astype(BF16), preferred_element_type=F32)
    tpos = feats[:, 0:1]
    live = feats[:, LANES - 1:LANES]
    val = filt * jnp.exp(-tpos * jnp.abs(dl_ref[...])) * live

    @pl.when((pl.program_id(0) == 0) & (pl.program_id(1) == 0))
    def _():
        ss_ref[...] = jnp.zeros_like(ss_ref)

    ss_ref[...] += jnp.sum(val * val, axis=0, keepdims=True)
    o_ref[...] = val.reshape(o_ref.shape).astype(BF16)


def _filter_time(feats, w1p, b1, f1, w2, b2, f2, w3, deltas, t2_blk=8):
    _, n1, _ = feats.shape
    c = w3.shape[1] // 2
    hid = w2.shape[0]
    const = lambda i, h: (0, 0)
    return pl.pallas_call(
        _filter_kernel,
        grid=(DFT_MINOR // t2_blk, 2),
        in_specs=[
            pl.BlockSpec((t2_blk, n1 // 2, LANES), lambda i, h: (i, h, 0)),
            pl.BlockSpec((LANES, hid), const),
            pl.BlockSpec((1, hid), const),
            pl.BlockSpec((1, hid), const),
            pl.BlockSpec((hid, hid), const),
            pl.BlockSpec((1, hid), const),
            pl.BlockSpec((1, hid), const),
            pl.BlockSpec((hid, c), lambda i, h: (0, h)),
            pl.BlockSpec((1, c), const),
        ],
        out_specs=[pl.BlockSpec((t2_blk, n1 // 2, c), lambda i, h: (i, h, 0)), pl.BlockSpec((1, c), const)],
        out_shape=[jax.ShapeDtypeStruct((DFT_MINOR, n1, c), BF16), jax.ShapeDtypeStruct((1, c), F32)],
        compiler_params=_params(("arbitrary", "arbitrary")),
        name="filter_time",
    )(feats, w1p, b1, f1, w2, b2, f2, w3, deltas)


def _dft_a_kernel(x_ref, tab_ref, o_ref):
    x = x_ref[...]
    x = x.reshape(-1, x.shape[-1])
    o_ref[0] = jnp.dot(tab_ref[0], x, preferred_element_type=F32).astype(o_ref.dtype)


def _dft_stage_a(x3, tab, n_out_rows, t2_major=False):
    if t2_major:
        _, r, c = x3.shape
        g = 1
        x_spec = pl.BlockSpec((1, r, c), lambda t2: (t2, 0, 0))
    else:
        g, r, wc = x3.shape
        c = wc // DFT_MINOR
        x_spec = pl.BlockSpec((g, r, c), lambda t2: (0, 0, t2))
    return pl.pallas_call(
        _dft_a_kernel,
        grid=(DFT_MINOR,),
        in_specs=[x_spec,
                  pl.BlockSpec((1, n_out_rows, g * r), lambda t2: (t2, 0, 0))],
        out_specs=pl.BlockSpec((1, n_out_rows, c), lambda t2: (t2, 0, 0)),
        out_shape=jax.ShapeDtypeStruct((DFT_MINOR, n_out_rows, c), BF16),
        compiler_params=_params(("parallel",)),
        name="dft_stage_a",
    )(x3, tab)


def _filt_b_kernel(re_ref, im_ref, mf_ref, ss_ref, o_ref, *, n_total):
    rhs = jnp.concatenate([re_ref[...], im_ref[...]], axis=0)
    spec = jnp.dot(mf_ref[...], rhs, preferred_element_type=F32)
    scale = lax.rsqrt(ss_ref[...] + FILTER_EPS) * (1.0 / n_total)
    o_ref[0] = (spec * scale).astype(o_ref.dtype)


def _filter_stage_b(a_arr, mf, sumsq, n1):
    _, rows, c = a_arr.shape
    a2 = a_arr.reshape(DFT_MINOR, rows * c)
    n_total = n1 * DFT_MINOR
    return pl.pallas_call(
        functools.partial(_filt_b_kernel, n_total=n_total),
        grid=(n1,),
        in_specs=[pl.BlockSpec((DFT_MINOR, c), lambda k1: (0, k1)),
                  pl.BlockSpec((DFT_MINOR, c), lambda k1: (0, n1 + k1)),
                  pl.BlockSpec((2 * DFT_MINOR, 2 * DFT_MINOR), lambda k1: (0, 0)),
                  pl.BlockSpec((1, c), lambda k1: (0, 0))],
        out_specs=pl.BlockSpec((1, 2 * DFT_MINOR, c), lambda k1: (k1, 0, 0)),
        out_shape=jax.ShapeDtypeStruct((n1, 2 * DFT_MINOR, c), BF16),
        compiler_params=_params(("parallel",)),
        name="filter_stage_b",
    )(a2, a2, mf, sumsq)


def _conv_b_kernel(re_ref, im_ref, mf_ref, mi_ref, h_ref, o_ref):
    half = DFT_MINOR
    rhs = jnp.concatenate([re_ref[...], im_ref[...]], axis=0)
    spec = jnp.dot(mf_ref[...], rhs, preferred_element_type=F32)
    hh = h_ref[0].astype(F32)
    xr, xi = spec[:half], spec[half:]
    hr, hi = hh[:half], hh[half:]
    yr = xr * hr - xi * hi
    yi = xr * hi + xi * hr
    y = jnp.concatenate([yr, yi], axis=0).astype(BF16)
    o_ref[0] = jnp.dot(mi_ref[...], y, preferred_element_type=F32).astype(o_ref.dtype)


def _conv_stage_b(a_arr, mf, mi, hspec, n1):
    _, rows, c = a_arr.shape
    a2 = a_arr.reshape(DFT_MINOR, rows * c)
    return pl.pallas_call(
        _conv_b_kernel,
        grid=(n1,),
        in_specs=[pl.BlockSpec((DFT_MINOR, c), lambda k1: (0, k1)),
                  pl.BlockSpec((DFT_MINOR, c), lambda k1: (0, n1 + k1)),
                  pl.BlockSpec((2 * DFT_MINOR, 2 * DFT_MINOR), lambda k1: (0, 0)),
                  pl.BlockSpec((2 * DFT_MINOR, 2 * DFT_MINOR), lambda k1: (0, 0)),
                  pl.BlockSpec((1, 2 * DFT_MINOR, c), lambda k1: (k1, 0, 0))],
        out_specs=pl.BlockSpec((1, 2 * DFT_MINOR, c), lambda k1: (k1, 0, 0)),
        out_shape=jax.ShapeDtypeStruct((n1, 2 * DFT_MINOR, c), BF16),
        compiler_params=_params(("parallel",)),
        name="conv_stage_b",
    )(a2, a2, mf, mi, hspec)


def _conv_out_kernel(re_ref, im_ref, tab_ref, o_ref):
    rhs = jnp.concatenate([re_ref[...], im_ref[...]], axis=0)
    y = jnp.dot(tab_ref[0], rhs, preferred_element_type=F32)
    o_ref[...] = y.reshape(o_ref.shape).astype(o_ref.dtype)


def _conv_stage_out(b_arr, tab, batch, n1):
    _, rows, c = b_arr.shape
    b2 = b_arr.reshape(n1, rows * c)
    r = n1 // 2
    return pl.pallas_call(
        _conv_out_kernel,
        grid=(DFT_MINOR,),
        in_specs=[pl.BlockSpec((n1, c), lambda t2: (0, t2)),
                  pl.BlockSpec((n1, c), lambda t2: (0, DFT_MINOR + t2)),
                  pl.BlockSpec((1, batch * r, 2 * n1), lambda t2: (t2, 0, 0))],
        out_specs=pl.BlockSpec((batch, r, c), lambda t2: (0, 0, t2)),
        out_shape=jax.ShapeDtypeStruct((batch, r, DFT_MINOR * c), BF16),
        compiler_params=_params(("parallel",)),
        name="conv_stage_out",
    )(b2, b2, tab)


def _dft_tables(n1, n_sig_rows):
    n = n1 * DFT_MINOR
    k1 = np.arange(n1, dtype=np.int64)[None, :, None]
    t2 = np.arange(DFT_MINOR, dtype=np.int64)[:, None, None]

    def twiddled(n_t1):
        t1 = np.arange(n_t1, dtype=np.int64)[None, None, :]
        m = (k1 * (DFT_MINOR * t1 + t2)) % n
        ang = m.astype(np.float64) * (-2.0 * math.pi / n)
        return np.cos(ang).astype(np.float32), np.sin(ang).astype(np.float32)

    pr, pi = twiddled(n_sig_rows)
    tab_sig = np.concatenate([np.concatenate([pr, -pi], axis=2), np.concatenate([pi, pr], axis=2)], axis=1)
    prt, pit = np.swapaxes(pr, 1, 2), np.swapaxes(pi, 1, 2)
    tab_out = np.concatenate([np.concatenate([prt, pit], axis=2), np.concatenate([-pit, prt], axis=2)], axis=1)
    fr, fi = twiddled(n1)
    tab_filt = np.concatenate([fr, fi], axis=1)
    j = np.arange(DFT_MINOR)
    ang = -2.0 * np.pi * ((j[:, None] * j[None, :]) % DFT_MINOR) / DFT_MINOR
    cr, ci = np.cos(ang), np.sin(ang)
    mf = np.block([[cr, -ci], [ci, cr]])
    mi = np.block([[cr, ci], [-ci, cr]])
    return tuple(jnp.asarray(a, BF16) for a in (tab_sig, tab_out, tab_filt, mf, mi))


def _merge_a_kernel(a_ref, yc_ref, z_ref, x0_ref, d_ref, ga_ref, gh_ref, bga_ref, bgh_ref, wa_ref, wh_ref, o_ref, hy_scr):
    @pl.when(pl.program_id(1) == 0)
    def _():
        z = z_ref[...].astype(F32)
        hy = x0_ref[...].astype(F32) * (yc_ref[...].astype(F32) + z * d_ref[...])
        hy_scr[...] = hy.astype(BF16)

    ya = jnp.dot(a_ref[...], wa_ref[...], preferred_element_type=F32)
    yh = jnp.dot(hy_scr[...], wh_ref[...], preferred_element_type=F32)
    ga = jax.nn.sigmoid(ga_ref[...].astype(F32) + bga_ref[...])
    gh = jax.nn.sigmoid(gh_ref[...].astype(F32) + bgh_ref[...])
    o_ref[...] = (ga * ya + gh * yh).astype(BF16)


def _merge_a(attn, yconv, z, x0c, d_bias, proj, b_gate2, wa, wh, gate_col0, tm=512, tn=512):
    t, aw = attn.shape
    hw = yconv.shape[1]
    d = wa.shape[1]
    assert gate_col0 % tn == 0 and d % tn == 0
    gb = gate_col0 // tn
    nj = d // tn
    hy_spec = pl.BlockSpec((tm, hw), lambda i, j: (i, 0))
    return pl.pallas_call(
        _merge_a_kernel,
        grid=(t // tm, nj),
        in_specs=[pl.BlockSpec((tm, aw), lambda i, j: (i, 0)),
                  hy_spec, hy_spec, hy_spec,
                  pl.BlockSpec((1, hw), lambda i, j: (0, 0)),
                  pl.BlockSpec((tm, tn), lambda i, j: (i, gb + j)),
                  pl.BlockSpec((tm, tn), lambda i, j: (i, gb + nj + j)),
                  pl.BlockSpec((1, tn), lambda i, j: (0, j)),
                  pl.BlockSpec((1, tn), lambda i, j: (0, nj + j)),
                  pl.BlockSpec((aw, tn), lambda i, j: (0, j)),
                  pl.BlockSpec((hw, tn), lambda i, j: (0, j))],
        out_specs=pl.BlockSpec((tm, tn), lambda i, j: (i, j)),
        out_shape=jax.ShapeDtypeStruct((t, d), BF16),
        scratch_shapes=[pltpu.VMEM((tm, hw), BF16)],
        compiler_params=_params(("parallel", "arbitrary")),
        name="merge_gated",
    )(attn, yconv, z, x0c, d_bias, proj, proj, b_gate2, b_gate2, wa, wh)


def _merge_b_kernel(u_ref, x_ref, gi_ref, bi_ref, wo_ref, g1_ref, b1_ref, wr_ref, br_ref, h_ref, lg_ref):
    mixed = jnp.dot(u_ref[...], wo_ref[...], preferred_element_type=F32)
    h0 = _layer_norm(x_ref[...], gi_ref[...], bi_ref[...])
    h1 = _layer_norm(DEEPNORM_ALPHA * h0 + mixed, g1_ref[...], b1_ref[...])
    h_ref[...] = h1
    wr = wr_ref[...]
    h_hi = h1.astype(BF16)
    h_lo = (h1 - h_hi.astype(F32)).astype(BF16)
    w_hi = wr.astype(BF16)
    w_lo = (wr - w_hi.astype(F32)).astype(BF16)
    lg = jnp.dot(h_hi, w_hi, preferred_element_type=F32)
    lg += jnp.dot(h_hi, w_lo, preferred_element_type=F32)
    lg += jnp.dot(h_lo, w_hi, preferred_element_type=F32)
    lg_ref[...] = lg + br_ref[...]


def _merge_b(u, x2, gi, bi, wo, g1, b1, wr, br, tm=512):
    t, d = x2.shape
    const = lambda i: (0, 0)
    return pl.pallas_call(
        _merge_b_kernel,
        grid=(t // tm,),
        in_specs=[pl.BlockSpec((tm, d), lambda i: (i, 0)),
                  pl.BlockSpec((tm, d), lambda i: (i, 0)),
                  pl.BlockSpec((1, d), const), pl.BlockSpec((1, d), const),
                  pl.BlockSpec((d, d), const),
                  pl.BlockSpec((1, d), const), pl.BlockSpec((1, d), const),
                  pl.BlockSpec((d, LANES), const), pl.BlockSpec((1, LANES), const)],
        out_specs=[pl.BlockSpec((tm, d), lambda i: (i, 0)), pl.BlockSpec((tm, LANES), lambda i: (i, 0))],
        out_shape=[jax.ShapeDtypeStruct((t, d), F32), jax.ShapeDtypeStruct((t, LANES), F32)],
        compiler_params=_params(("parallel",)),
        name="merge_out_ln1",
    )(u, x2, gi, bi, wo, g1, b1, wr, br)


def _route_kernel(lg_ref, tri_ref, id_ref, gt_ref, cnt_ref, pre_scr):
    x = lg_ref[...]
    lane = lax.broadcasted_iota(jnp.int32, x.shape, 1).astype(F32)
    big = jnp.float32(1 << 20)
    neg = -jnp.inf
    cmask = lane < N_GROUPS
    cm = jnp.max(jnp.where(cmask, x, neg), axis=-1, keepdims=True)
    grp = jnp.min(jnp.where(cmask & (x == cm), lane, big), axis=-1, keepdims=True)
    csum = jnp.sum(jnp.where(cmask, jnp.exp(x - cm), 0.0), axis=-1, keepdims=True)
    p_grp = 1.0 / csum
    lo = N_GROUPS + grp * EXPERTS_PER_GROUP
    fmask = (lane >= lo) & (lane < lo + EXPERTS_PER_GROUP)
    f1 = jnp.max(jnp.where(fmask, x, neg), axis=-1, keepdims=True)
    i1 = jnp.min(jnp.where(fmask & (x == f1), lane, big), axis=-1, keepdims=True)
    mask2 = fmask & (lane != i1)
    f2 = jnp.max(jnp.where(mask2, x, neg), axis=-1, keepdims=True)
    i2 = jnp.min(jnp.where(mask2 & (x == f2), lane, big), axis=-1, keepdims=True)
    e2 = jnp.exp(f2 - f1)
    g1 = p_grp / (1.0 + e2)
    g2 = p_grp * e2 / (1.0 + e2)
    gt_ref[...] = jnp.where(lane == 0, g1, jnp.where(lane == 1, g2, 0.0))

    @pl.when(pl.program_id(0) == 0)
    def _():
        cnt_ref[...] = jnp.zeros_like(cnt_ref)

    sel1 = lane == i1
    sel2 = lane == i2
    chosen = jnp.where(sel1 | sel2, 1.0, 0.0)
    sub = tri_ref.shape[0]
    carry = cnt_ref[...]
    for s in range(x.shape[0] // sub):
        cs = chosen[s * sub:(s + 1) * sub]
        pre_scr[s * sub:(s + 1) * sub, :] = jnp.dot(tri_ref[...], cs.astype(BF16), preferred_element_type=F32) + carry
        carry = carry + jnp.sum(cs, axis=0, keepdims=True)
    cnt_ref[...] = carry
    before = pre_scr[...]
    r1 = jnp.sum(jnp.where(sel1, before, 0.0), axis=-1, keepdims=True)
    r2 = jnp.sum(jnp.where(sel2, before, 0.0), axis=-1, keepdims=True)
    ids = jnp.where(lane == 0, i1 - N_GROUPS, jnp.where(lane == 1, i2 - N_GROUPS,
                    jnp.where(lane == 2, r1, jnp.where(lane == 3, r2, 0.0))))
    id_ref[...] = ids.astype(jnp.int32)


def _route(logits, tm=2048, sub=256):
    t = logits.shape[0]
    spec = pl.BlockSpec((tm, LANES), lambda i: (i, 0))
    tri = jnp.asarray(np.tril(np.ones((sub, sub), np.float32), -1), BF16)
    return pl.pallas_call(
        _route_kernel,
        grid=(t // tm,),
        in_specs=[spec, pl.BlockSpec((sub, sub), lambda i: (0, 0))],
        out_specs=[spec, spec, pl.BlockSpec((1, LANES), lambda i: (0, 0))],
        out_shape=[jax.ShapeDtypeStruct((t, LANES), jnp.int32), jax.ShapeDtypeStruct((t, LANES), F32),
                   jax.ShapeDtypeStruct((1, LANES), F32)],
        scratch_shapes=[pltpu.VMEM((tm, LANES), F32)],
        compiler_params=_params(("arbitrary",)),
        name="route_topk",
    )(logits, tri)


def _invert_kernel(d0_ref, d1_ref, o_ref):
    n = o_ref.shape[0]
    n_tok = d0_ref.shape[0]

    def zero(i, c):
        o_ref[i] = 0
        return c
    lax.fori_loop(0, n, zero, 0, unroll=8)

    def put(t, c):
        o_ref[d0_ref[t]] = t
        o_ref[d1_ref[t]] = t
        return c
    lax.fori_loop(0, n_tok, put, 0, unroll=8)


def _invert(dest0, dest1, n_rows):
    return pl.pallas_call(
        _invert_kernel,
        grid_spec=pltpu.PrefetchScalarGridSpec(
            num_scalar_prefetch=2, grid=(1,), in_specs=[],
            out_specs=pl.BlockSpec(memory_space=pltpu.SMEM)),
        out_shape=jax.ShapeDtypeStruct((n_rows,), jnp.int32),
        compiler_params=_params(("arbitrary",)),
        name="dispatch_invert",
    )(dest0, dest1)


def _row_gather(src_hbm, idx_ref, base, dst, sem, n_rows):
    for r in range(n_rows):
        tok = idx_ref[base + r]
        pltpu.make_async_copy(src_hbm.at[pl.ds(tok, 1)], dst.at[pl.ds(r, 1)], sem).start()


def _row_gather_wait(src_hbm, dst, sem, n_rows):
    for r in range(n_rows):
        pltpu.make_async_copy(src_hbm.at[pl.ds(0, 1)], dst.at[pl.ds(r, 1)], sem).wait()


def _moe_kernel(be_ref, nb_ref, tok_ref, h_hbm, wg_ref, wu_ref, wd_ref, o_ref, xbuf, sem):
    i = pl.program_id(0)
    n_used = nb_ref[0]
    slot = i % 2
    rows = xbuf.shape[1]

    @pl.when(i == 0)
    def _():
        _row_gather(h_hbm, tok_ref, 0, xbuf.at[0], sem.at[0], rows)

    @pl.when(i < n_used)
    def _():
        _row_gather(h_hbm, tok_ref, (i + 1) * rows, xbuf.at[1 - slot], sem.at[1 - slot], rows)
        _row_gather_wait(h_hbm, xbuf.at[slot], sem.at[slot], rows)
        x = xbuf[slot].astype(BF16)
        g = jnp.dot(x, wg_ref[0], preferred_element_type=F32)
        u = jnp.dot(x, wu_ref[0], preferred_element_type=F32)
        hdn = (g * jax.nn.sigmoid(g) * u).astype(BF16)
        o_ref[...] = jnp.dot(hdn, wd_ref[0], preferred_element_type=F32)

    @pl.when(i == n_used)
    def _():
        _row_gather_wait(h_hbm, xbuf.at[slot], sem.at[slot], rows)

    @pl.when(i >= n_used)
    def _():
        o_ref[...] = jnp.zeros_like(o_ref)


def _moe_ffn(blk_expert, n_used, row_tok, h1, wg, wu, wd, rows=MOE_ROWS):
    t, d = h1.shape
    f = wg.shape[2]
    n_blk = blk_expert.shape[0]
    assert row_tok.shape[0] == (n_blk + 1) * rows
    grid_spec = pltpu.PrefetchScalarGridSpec(
        num_scalar_prefetch=3,
        grid=(n_blk,),
        in_specs=[pl.BlockSpec(memory_space=pl.ANY),
                  pl.BlockSpec((1, d, f), lambda i, be, nb, tk: (be[i], 0, 0)),
                  pl.BlockSpec((1, d, f), lambda i, be, nb, tk: (be[i], 0, 0)),
                  pl.BlockSpec((1, f, d), lambda i, be, nb, tk: (be[i], 0, 0))],
        out_specs=pl.BlockSpec((rows, d), lambda i, be, nb, tk: (i, 0)),
        scratch_shapes=[pltpu.VMEM((2, rows, d), F32), pltpu.SemaphoreType.DMA((2,))],
    )
    return pl.pallas_call(
        _moe_kernel,
        grid_spec=grid_spec,
        out_shape=jax.ShapeDtypeStruct((n_blk * rows, d), F32),
        compiler_params=_params(("arbitrary",)),
        name="moe_ffn",
    )(blk_expert, n_used, row_tok, h1, wg, wu, wd)


def _combine_kernel(p0_ref, p1_ref, ys_hbm, h_ref, gt_ref, g2_ref, b2_ref, o_ref, ybuf, sem):
    i = pl.program_id(0)
    n = pl.num_programs(0)
    slot = i % 2
    tm = h_ref.shape[0]

    def issue(step, s):
        _row_gather(ys_hbm, p0_ref, step * tm, ybuf.at[s, 0], sem.at[s], tm)
        _row_gather(ys_hbm, p1_ref, step * tm, ybuf.at[s, 1], sem.at[s], tm)

    def drain(s):
        _row_gather_wait(ys_hbm, ybuf.at[s, 0], sem.at[s], tm)
        _row_gather_wait(ys_hbm, ybuf.at[s, 1], sem.at[s], tm)

    @pl.when(i == 0)
    def _():
        issue(0, 0)

    issue(i + 1, 1 - slot)
    drain(slot)
    gt = gt_ref[...]
    moe = gt[:, 0:1] * ybuf[slot, 0] + gt[:, 1:2] * ybuf[slot, 1]
    o_ref[...] = _layer_norm(DEEPNORM_ALPHA * h_ref[...] + moe, g2_ref[...], b2_ref[...])

    @pl.when(i == n - 1)
    def _():
        drain(1 - slot)


def _combine(pos0, pos1, ys, h1, gates, g2, b2, tm=256):
    t, d = h1.shape
    assert pos0.shape[0] == t + tm and pos1.shape[0] == t + tm
    const = lambda i, a, b: (0, 0)
    grid_spec = pltpu.PrefetchScalarGridSpec(
        num_scalar_prefetch=2,
        grid=(t // tm,),
        in_specs=[pl.BlockSpec(memory_space=pl.ANY),
                  pl.BlockSpec((tm, d), lambda i, a, b: (i, 0)),
                  pl.BlockSpec((tm, LANES), lambda i, a, b: (i, 0)),
                  pl.BlockSpec((1, d), const), pl.BlockSpec((1, d), const)],
        out_specs=pl.BlockSpec((tm, d), lambda i, a, b: (i, 0)),
        scratch_shapes=[pltpu.VMEM((2, 2, tm, d), F32), pltpu.SemaphoreType.DMA((2,))],
    )
    return pl.pallas_call(
        _combine_kernel,
        grid_spec=grid_spec,
        out_shape=jax.ShapeDtypeStruct((t, d), F32),
        compiler_params=_params(("arbitrary",)),
        name="moe_combine_ln2",
    )(pos0, pos1, ys, h1, gates, g2, b2)


def _rope_tables(seq):
    rows = seq // GRID_W
    row_idx = jnp.repeat(jnp.arange(rows, dtype=jnp.int32), GRID_W).astype(F32)
    col_idx = jnp.tile(jnp.arange(GRID_W, dtype=jnp.int32), rows).astype(F32)
    half = HEAD_DIM // 2
    inv_freq = ROPE_THETA ** (-jnp.arange(0, half, 2, dtype=F32) / half)
    ang_r = row_idx[:, None] * inv_freq[None, :]
    ang_c = col_idx[:, None] * inv_freq[None, :]
    cr, sr, cc, sc = jnp.cos(ang_r), jnp.sin(ang_r), jnp.cos(ang_c), jnp.sin(ang_c)
    cs = jnp.concatenate([cr, cc, cr, cc], axis=-1)
    sn = jnp.concatenate([-sr, -sc, sr, sc], axis=-1)
    return cs, sn


def _pair_split(v):
    lead = v.shape[:-1]
    q4 = v.reshape(lead + (-1, 2, 2, HEAD_DIM // 4))
    return jnp.swapaxes(q4, -3, -2).reshape(v.shape)


def _filter_features(seq):
    n1 = 2 * seq // DFT_MINOR
    n = (DFT_MINOR * np.arange(n1)[None, :] + np.arange(DFT_MINOR)[:, None]).astype(np.int64)
    j = np.where(n <= seq, n, 2 * seq - n)
    j = np.where(n == seq, 0, j).astype(np.float64)
    t = (j / (seq - 1))[..., None]
    w = (2.0 * math.pi * j / seq)[..., None]
    bands = np.linspace(1e-4, FILTER_BANDS - 1, FILTER_BANDS, dtype=np.float32).astype(np.float64)
    feats = np.zeros(n.shape + (LANES,), np.float32)
    feats[..., 0:1] = t
    feats[..., 1:1 + FILTER_BANDS] = np.cos(bands * w)
    feats[..., 1 + FILTER_BANDS:FILTER_EMB] = -np.sin(bands * w)
    feats[..., LANES - 1] = (n != seq)
    return jnp.asarray(feats)


def _dispatch_plan(ids, counts, n_tok, rows, tail):
    counts = counts.astype(jnp.int32)
    padded = (counts + rows - 1) // rows * rows
    pad_end = jnp.cumsum(padded)
    pad_start = pad_end - padded
    n_blk = (n_tok * TOP_K + N_EXPERTS * rows) // rows + 1
    blk_start = jnp.arange(n_blk, dtype=jnp.int32) * rows
    blk_expert = jnp.minimum(jnp.sum((pad_end[None, :] <= blk_start[:, None]).astype(jnp.int32), axis=1),
                             N_EXPERTS - 1)
    n_used = (pad_end[-1] // rows).astype(jnp.int32).reshape(1)
    experts = jnp.arange(N_EXPERTS, dtype=jnp.int32)[None, :]

    def dest(e, rank):
        start = jnp.sum(jnp.where(e[:, None] == experts, pad_start[None, :], 0), axis=1)
        return jnp.pad(start + rank, (0, tail))

    pos0 = dest(ids[:, 0], ids[:, 2])
    pos1 = dest(ids[:, 1], ids[:, 3])
    row_tok = _invert(pos0[:n_tok], pos1[:n_tok], (n_blk + 1) * rows)
    return blk_expert, n_used, row_tok, pos0, pos1


def kernel(x, ln_in_g, ln_in_b, w_in, b_gate, q_norm_g, k_norm_g, hy_conv_w, hy_conv_b, filt_w1, filt_b1, filt_f1, filt_w2, filt_b2, filt_f2, filt_w3, hy_bias_d, w_attn_o, w_hy_o, w_out, ln1_g, ln1_b, w_route_grp, b_route_grp, w_route_exp, b_route_exp, w_exp_gate, w_exp_up, w_exp_down, ln2_g, ln2_b):
    batch, seq, d = x.shape
    assert batch == 2, "the long convolution packs exactly two batch rows as one complex signal"
    t = batch * seq
    hw = hy_bias_d.shape[1]
    l = 0
    x2 = x.reshape(t, d)
    row = lambda v: v.reshape(1, -1)

    cs, sn = _rope_tables(seq)
    qg = row(q_norm_g[l]) * (HEAD_DIM ** -0.5 * math.log2(math.e))
    qg = _pair_split(qg)
    kg = _pair_split(row(k_norm_g[l]))
    n_qk = ATTN_WIDTH + KV_WIDTH
    w_bf = jnp.concatenate([_pair_split(w_in[l][:, :n_qk]).astype(BF16), w_in[l][:, n_qk:].astype(BF16)], axis=1)
    proj = _ln_inproj(x2, row(ln_in_g), row(ln_in_b), w_bf, qg, kg, cs, sn, seq)

    attn = _attention(proj, batch, seq)

    hy_col0 = ATTN_WIDTH + 2 * KV_WIDTH
    z, x0c = _hy_pre(proj, hy_conv_w[l], row(hy_conv_b[l]), batch, seq, hw, hy_col0)
    n1 = 2 * seq // DFT_MINOR
    tab_sig, tab_out, tab_filt, mf, mi = _dft_tables(n1, n1 // 2)
    feats = _filter_features(seq)
    w1p = jnp.pad(filt_w1[l], ((0, LANES - FILTER_EMB), (0, 0)))
    min_decay = math.log(DECAY_TARGET) / SLOW_DECAY_PCT
    max_decay = math.log(DECAY_TARGET) / FAST_DECAY_PCT
    deltas = jnp.linspace(min_decay, max_decay, hw, dtype=F32)[None, :]
    two_sided, sumsq = _filter_time(feats, w1p, row(filt_b1[l]), row(filt_f1[l]), filt_w2[l], row(filt_b2[l]),
                                    row(filt_f2[l]), filt_w3[l], deltas)
    fa = _dft_stage_a(two_sided, tab_filt, 2 * n1, t2_major=True)
    hspec = _filter_stage_b(fa, mf, sumsq, n1)
    za = _dft_stage_a(z.reshape(batch, n1 // 2, DFT_MINOR * hw), tab_sig, 2 * n1)
    zb = _conv_stage_b(za, mf, mi, hspec, n1)
    yconv = _conv_stage_out(zb, tab_out, batch, n1).reshape(t, hw)

    gate_col0 = hy_col0 + 3 * hw
    u = _merge_a(attn, yconv, z, x0c, row(hy_bias_d[l]), proj, row(b_gate[l]), w_attn_o[l].astype(BF16),
                 w_hy_o[l].astype(BF16), gate_col0)
    n_r = N_GROUPS + N_EXPERTS
    wr = jnp.pad(jnp.concatenate([w_route_grp[l], w_route_exp[l]], axis=1), ((0, 0), (0, LANES - n_r)))
    br = jnp.pad(jnp.concatenate([b_route_grp[l], b_route_exp[l]]), (0, LANES - n_r)).reshape(1, LANES)
    h1, logits = _merge_b(u, x2, row(ln_in_g), row(ln_in_b), w_out[l].astype(BF16), row(ln1_g[l]), row(ln1_b[l]),
                          wr, br)

    ids, gates, cnt = _route(logits)
    counts = cnt[0, N_GROUPS:N_GROUPS + N_EXPERTS]
    combine_tile = 256
    blk_expert, n_used, row_tok, pos0, pos1 = _dispatch_plan(ids, counts, t, MOE_ROWS, combine_tile)
    ys = _moe_ffn(blk_expert, n_used, row_tok, h1, w_exp_gate[l].astype(BF16), w_exp_up[l].astype(BF16),
                  w_exp_down[l].astype(BF16))
    out = _combine(pos0, pos1, ys, h1, gates, row(ln2_g[l]), row(ln2_b[l]), tm=combine_tile)
    return out.reshape(batch, seq, d)
```

```python
import functools
import math

import jax
import jax.numpy as jnp
import numpy as np
from jax import lax
from jax.experimental import pallas as pl
from jax.experimental.pallas import tpu as pltpu

F32 = jnp.float32
BF16 = jnp.bfloat16

GRID_W = 64
N_Q_HEADS = 8
N_KV_HEADS = 2
HEAD_DIM = 128
Q_GROUP = N_Q_HEADS // N_KV_HEADS
ATTN_WIDTH = N_Q_HEADS * HEAD_DIM
KV_WIDTH = N_KV_HEADS * HEAD_DIM
ROPE_THETA = 10000.0
QK_EPS = 1e-6
SHORT_CONV = 3
FILTER_BANDS = 16
FILTER_EMB = 1 + 2 * FILTER_BANDS
DECAY_TARGET = 1e-2
FAST_DECAY_PCT = 0.3
SLOW_DECAY_PCT = 1.5
FILTER_EPS = 1e-6
N_GROUPS = 4
EXPERTS_PER_GROUP = 8
N_EXPERTS = N_GROUPS * EXPERTS_PER_GROUP
TOP_K = 2
LN_EPS = 1e-5
DEPTH = 1
DEEPNORM_ALPHA = (2 * DEPTH) ** 0.25

LANES = 128
V7X_VMEM_LIMIT = 56 * 1024 * 1024
DFT_MINOR = 128

MOE_ROWS = 256


def _params(sem, vmem=V7X_VMEM_LIMIT):
    return pltpu.CompilerParams(dimension_semantics=sem, vmem_limit_bytes=vmem)


def _layer_norm(x, g, b):
    mu = jnp.mean(x, axis=-1, keepdims=True)
    xc = x - mu
    var = jnp.mean(xc * xc, axis=-1, keepdims=True)
    return xc * lax.rsqrt(var + LN_EPS) * g + b


def _norm_rope(acc, gain, cs, sn):
    ms = jnp.mean(acc * acc, axis=-1, keepdims=True)
    xn = acc * lax.rsqrt(ms + QK_EPS) * gain
    return xn * cs + pltpu.roll(xn, HEAD_DIM // 2, 1) * sn


def _ln_inproj_kernel(x_ref, g_ref, b_ref, w_ref, qg_ref, kg_ref, cs_ref, sn_ref, o_ref, h_scr, *, tn):
    j = pl.program_id(1)
    nchunk = tn // HEAD_DIM
    n_q = ATTN_WIDTH // HEAD_DIM
    n_qk = n_q + N_KV_HEADS

    @pl.when(j == 0)
    def _():
        h = _layer_norm(x_ref[...], g_ref[...], b_ref[...])
        h_scr[...] = h.astype(BF16)

    acc = jnp.dot(h_scr[...], w_ref[...], preferred_element_type=F32)

    def store(first_plain_chunk_fn):
        for c in range(nchunk):
            sl = slice(c * HEAD_DIM, (c + 1) * HEAD_DIM)
            kind = first_plain_chunk_fn(c)
            if kind == "q":
                o_ref[:, sl] = _norm_rope(acc[:, sl], qg_ref[...], cs_ref[...], sn_ref[...]).astype(BF16)
            elif kind == "k":
                o_ref[:, sl] = _norm_rope(acc[:, sl], kg_ref[...], cs_ref[...], sn_ref[...]).astype(BF16)
            else:
                o_ref[:, sl] = acc[:, sl].astype(BF16)

    n_q_tiles = n_q // nchunk
    assert n_q % nchunk == 0 and N_KV_HEADS <= nchunk

    @pl.when(j < n_q_tiles)
    def _():
        store(lambda c: "q")

    @pl.when(j == n_q_tiles)
    def _():
        store(lambda c: "k" if c < N_KV_HEADS else "p")

    @pl.when(j > n_q_tiles)
    def _():
        store(lambda c: "p")


def _ln_inproj(x2, g, b, w_bf, qg, kg, cs, sn, seq, tm=1024, tn=512):
    t, d = x2.shape
    n = w_bf.shape[1]
    assert t % tm == 0 and n % tn == 0 and seq % tm == 0
    pos_blocks = seq // tm
    return pl.pallas_call(
        functools.partial(_ln_inproj_kernel, tn=tn),
        grid=(t // tm, n // tn),
        in_specs=[
            pl.BlockSpec((tm, d), lambda i, j: (i, 0)),
            pl.BlockSpec((1, d), lambda i, j: (0, 0)),
            pl.BlockSpec((1, d), lambda i, j: (0, 0)),
            pl.BlockSpec((d, tn), lambda i, j: (0, j)),
            pl.BlockSpec((1, HEAD_DIM), lambda i, j: (0, 0)),
            pl.BlockSpec((1, HEAD_DIM), lambda i, j: (0, 0)),
            pl.BlockSpec((tm, HEAD_DIM), lambda i, j: (i % pos_blocks, 0)),
            pl.BlockSpec((tm, HEAD_DIM), lambda i, j: (i % pos_blocks, 0)),
        ],
        out_specs=pl.BlockSpec((tm, tn), lambda i, j: (i, j)),
        out_shape=jax.ShapeDtypeStruct((t, n), BF16),
        scratch_shapes=[pltpu.VMEM((tm, d), BF16)],
        compiler_params=_params(("parallel", "arbitrary")),
        name="ln_inproj",
    )(x2, g, b, w_bf, qg, kg, cs, sn)


def _attn_kernel(q_ref, k_ref, v_ref, o_ref, vt_scr, *, tq, tk):
    seq = k_ref.shape[0]
    n_chunks = seq // tk
    m_cols = Q_GROUP * tq

    @pl.when(pl.program_id(2) == 0)
    def _():
        for c in range(n_chunks):
            vt_scr[c] = v_ref[c * tk:(c + 1) * tk, :].T

    qt = jnp.concatenate([q_ref[:, h * HEAD_DIM:(h + 1) * HEAD_DIM].T for h in range(Q_GROUP)], axis=1)

    def scores(c):
        start = pl.multiple_of(c * tk, tk)
        return jnp.dot(k_ref[pl.ds(start, tk), :], qt, preferred_element_type=F32)

    def update(c, s, m, l, acc):
        m_new = jnp.maximum(m, jnp.max(s, axis=0, keepdims=True))
        p = jnp.exp2(s - m_new)
        alpha = jnp.exp2(m - m_new)
        l_new = alpha * l + jnp.sum(p, axis=0, keepdims=True)
        acc_new = alpha * acc + jnp.dot(vt_scr[c], p.astype(BF16), preferred_element_type=F32)
        return m_new, l_new, acc_new

    def body(c, carry):
        m, l, acc, s = carry
        s_next = scores(jnp.minimum(c + 1, n_chunks - 1))
        m, l, acc = update(c, s, m, l, acc)
        return m, l, acc, s_next

    m0 = jnp.full((1, m_cols), -jnp.inf, F32)
    l0 = jnp.zeros((1, m_cols), F32)
    a0 = jnp.zeros((HEAD_DIM, m_cols), F32)
    _, l, acc, _ = lax.fori_loop(0, n_chunks, body, (m0, l0, a0, scores(0)), unroll=8)
    out = acc / l
    for h in range(Q_GROUP):
        o_ref[:, h * HEAD_DIM:(h + 1) * HEAD_DIM] = out[:, h * tq:(h + 1) * tq].T.astype(BF16)


def _attention(proj, batch, seq, tq=128, tk=512):
    t = proj.shape[0]
    nq = seq // tq
    gw = Q_GROUP * HEAD_DIM
    k_col0 = ATTN_WIDTH // HEAD_DIM
    v_col0 = (ATTN_WIDTH + KV_WIDTH) // HEAD_DIM
    return pl.pallas_call(
        functools.partial(_attn_kernel, tq=tq, tk=tk),
        grid=(batch, N_KV_HEADS, nq),
        in_specs=[
            pl.BlockSpec((tq, gw), lambda b, g, i: (b * nq + i, g)),
            pl.BlockSpec((seq, HEAD_DIM), lambda b, g, i: (b, k_col0 + g)),
            pl.BlockSpec((seq, HEAD_DIM), lambda b, g, i: (b, v_col0 + g)),
        ],
        out_specs=pl.BlockSpec((tq, gw), lambda b, g, i: (b * nq + i, g)),
        out_shape=jax.ShapeDtypeStruct((t, ATTN_WIDTH), BF16),
        scratch_shapes=[pltpu.VMEM((seq // tk, HEAD_DIM, tk), BF16)],
        compiler_params=_params(("parallel", "parallel", "arbitrary")),
        name="attention",
    )(proj, proj, proj)


def _hy_pre_kernel(x0_ref, x1_ref, hv_ref,
                   p0_ref, p1_ref, pv_ref, n0_ref, n1_ref, nv_ref,
                   w0_ref, w1_ref, wv_ref, b0_ref, b1_ref, bv_ref,
                   z_ref, x0c_ref):
    r = pl.program_id(2)
    last = pl.num_programs(2) - 1
    ts = x0_ref.shape[0]
    row = lax.broadcasted_iota(jnp.int32, x0_ref.shape, 0)

    def conv(x_ref, p_ref, n_ref, w_ref, b_ref):
        x = x_ref[...].astype(F32)
        prev_row = jnp.where(r == 0, 0.0, p_ref[7:8, :].astype(F32))
        next_row = jnp.where(r == last, 0.0, n_ref[0:1, :].astype(F32))
        up = jnp.where(row == 0, prev_row, pltpu.roll(x, 1, 0))
        dn = jnp.where(row == ts - 1, next_row, pltpu.roll(x, ts - 1, 0))
        w = w_ref[...]
        return b_ref[...] + up * w[0:1, :] + x * w[1:2, :] + dn * w[2:3, :]

    x0c = conv(x0_ref, p0_ref, n0_ref, w0_ref, b0_ref)
    x1c = conv(x1_ref, p1_ref, n1_ref, w1_ref, b1_ref)
    hvc = conv(hv_ref, pv_ref, nv_ref, wv_ref, bv_ref)
    z_ref[...] = (hvc * x1c).astype(BF16)
    x0c_ref[...] = x0c.astype(BF16)


def _hy_pre(proj, conv_w, conv_b, batch, seq, hw, col0, ts=512, tc=256):
    t = proj.shape[0]
    nr = seq // ts
    nct = hw // tc
    cb0 = col0 // tc
    halo = 8
    hb = ts // halo

    def main(off):
        return pl.BlockSpec((ts, tc), lambda b, c, r: (b * nr + r, cb0 + off * nct + c))

    def prev(off):
        return pl.BlockSpec((halo, tc), lambda b, c, r: (jnp.maximum((b * nr + r) * hb - 1, 0), cb0 + off * nct + c))

    def nxt(off):
        return pl.BlockSpec((halo, tc), lambda b, c, r: (jnp.minimum((b * nr + r + 1) * hb, t // halo - 1),
                                                       cb0 + off * nct + c))

    def wspec(off):
        return pl.BlockSpec((SHORT_CONV, tc), lambda b, c, r: (0, off * nct + c))

    def bspec(off):
        return pl.BlockSpec((1, tc), lambda b, c, r: (0, off * nct + c))

    out_spec = pl.BlockSpec((ts, tc), lambda b, c, r: (b * nr + r, c))
    return pl.pallas_call(
        _hy_pre_kernel,
        grid=(batch, nct, nr),
        in_specs=[main(0), main(1), main(2), prev(0), prev(1), prev(2), nxt(0), nxt(1), nxt(2),
                  wspec(0), wspec(1), wspec(2), bspec(0), bspec(1), bspec(2)],
        out_specs=[out_spec, out_spec],
        out_shape=[jax.ShapeDtypeStruct((t, hw), BF16), jax.ShapeDtypeStruct((t, hw), BF16)],
        compiler_params=_params(("parallel", "parallel", "parallel")),
        name="hy_pre",
    )(proj, proj, proj, proj, proj, proj, proj, proj, proj,
      conv_w, conv_w, conv_w, conv_b, conv_b, conv_b)


def _dot3(a, b):
    a_hi = a.astype(BF16)
    a_lo = (a - a_hi.astype(F32)).astype(BF16)
    b_hi = b.astype(BF16)
    b_lo = (b - b_hi.astype(F32)).astype(BF16)
    out = jnp.dot(a_hi, b_hi, preferred_element_type=F32)
    out += jnp.dot(a_hi, b_lo, preferred_element_type=F32)
    out += jnp.dot(a_lo, b_hi, preferred_element_type=F32)
    return out


def _filter_kernel(feat_ref, w1_ref, b1_ref, f1_ref, w2_ref, b2_ref, f2_ref, w3_ref, dl_ref, o_ref, ss_ref):
    feats = feat_ref[...]
    feats = feats.reshape(-1, feats.shape[-1])
    h = jnp.sin(f1_ref[...] * (_dot3(feats, w1_ref[...]) + b1_ref[...]))
    h = jnp.sin(f2_ref[...] * (_dot3(h, w2_ref[...]) + b2_ref[...]))
    filt = jnp.dot(h.astype(BF16), w3_ref[...].astype(BF16), preferred_element_type=F32)
    tpos = feats[:, 0:1]
    live = feats[:, LANES - 1:LANES]
    val = filt * jnp.exp(-tpos * jnp.abs(dl_ref[...])) * live

    @pl.when((pl.program_id(0) == 0) & (pl.program_id(1) == 0))
    def _():
        ss_ref[...] = jnp.zeros_like(ss_ref)

    ss_ref[...] += jnp.sum(val * val, axis=0, keepdims=True)
    o_ref[...] = val.reshape(o_ref.shape).astype(BF16)


def _filter_time(feats, w1p, b1, f1, w2, b2, f2, w3, deltas, t2_blk=8):
    _, n1, _ = feats.shape
    c = w3.shape[1] // 2
    hid = w2.shape[0]
    const = lambda i, h: (0, 0)
    return pl.pallas_call(
        _filter_kernel,
        grid=(DFT_MINOR // t2_blk, 2),
        in_specs=[
            pl.BlockSpec((t2_blk, n1 // 2, LANES), lambda i, h: (i, h, 0)),
            pl.BlockSpec((LANES, hid), const),
            pl.BlockSpec((1, hid), const),
            pl.BlockSpec((1, hid), const),
            pl.BlockSpec((hid, hid), const),
            pl.BlockSpec((1, hid), const),
            pl.BlockSpec((1, hid), const),
            pl.BlockSpec((hid, c), lambda i, h: (0, h)),
            pl.BlockSpec((1, c), const),
        ],
        out_specs=[pl.BlockSpec((t2_blk, n1 // 2, c), lambda i, h: (i, h, 0)), pl.BlockSpec((1, c), const)],
        out_shape=[jax.ShapeDtypeStruct((DFT_MINOR, n1, c), BF16), jax.ShapeDtypeStruct((1, c), F32)],
        compiler_params=_params(("arbitrary", "arbitrary")),
        name="filter_time",
    )(feats, w1p, b1, f1, w2, b2, f2, w3, deltas)


def _dft_a_kernel(x_ref, tab_ref, o_ref):
    x = x_ref[...]
    x = x.reshape(-1, x.shape[-1])
    o_ref[0] = jnp.dot(tab_ref[0], x, preferred_element_type=F32).astype(o_ref.dtype)


def _dft_stage_a(x3, tab, n_out_rows, t2_major=False):
    if t2_major:
        _, r, c = x3.shape
        g = 1
        x_spec = pl.BlockSpec((1, r, c), lambda t2: (t2, 0, 0))
    else:
        g, r, wc = x3.shape
        c = wc // DFT_MINOR
        x_spec = pl.BlockSpec((g, r, c), lambda t2: (0, 0, t2))
    return pl.pallas_call(
        _dft_a_kernel,
        grid=(DFT_MINOR,),
        in_specs=[x_spec,
                  pl.BlockSpec((1, n_out_rows, g * r), lambda t2: (t2, 0, 0))],
        out_specs=pl.BlockSpec((1, n_out_rows, c), lambda t2: (t2, 0, 0)),
        out_shape=jax.ShapeDtypeStruct((DFT_MINOR, n_out_rows, c), BF16),
        compiler_params=_params(("parallel",)),
        name="dft_stage_a",
    )(x3, tab)


def _filt_b_kernel(re_ref, im_ref, mf_ref, ss_ref, o_ref, *, n_total):
    rhs = jnp.concatenate([re_ref[...], im_ref[...]], axis=0)
    spec = jnp.dot(mf_ref[...], rhs, preferred_element_type=F32)
    scale = lax.rsqrt(ss_ref[...] + FILTER_EPS) * (1.0 / n_total)
    o_ref[0] = (spec * scale).astype(o_ref.dtype)


def _filter_stage_b(a_arr, mf, sumsq, n1):
    _, rows, c = a_arr.shape
    a2 = a_arr.reshape(DFT_MINOR, rows * c)
    n_total = n1 * DFT_MINOR
    return pl.pallas_call(
        functools.partial(_filt_b_kernel, n_total=n_total),
        grid=(n1,),
        in_specs=[pl.BlockSpec((DFT_MINOR, c), lambda k1: (0, k1)),
                  pl.BlockSpec((DFT_MINOR, c), lambda k1: (0, n1 + k1)),
                  pl.BlockSpec((2 * DFT_MINOR, 2 * DFT_MINOR), lambda k1: (0, 0)),
                  pl.BlockSpec((1, c), lambda k1: (0, 0))],
        out_specs=pl.BlockSpec((1, 2 * DFT_MINOR, c), lambda k1: (k1, 0, 0)),
        out_shape=jax.ShapeDtypeStruct((n1, 2 * DFT_MINOR, c), BF16),
        compiler_params=_params(("parallel",)),
        name="filter_stage_b",
    )(a2, a2, mf, sumsq)


def _conv_b_kernel(re_ref, im_ref, mf_ref, mi_ref, h_ref, o_ref):
    half = DFT_MINOR
    rhs = jnp.concatenate([re_ref[...], im_ref[...]], axis=0)
    spec = jnp.dot(mf_ref[...], rhs, preferred_element_type=F32)
    hh = h_ref[0].astype(F32)
    xr, xi = spec[:half], spec[half:]
    hr, hi = hh[:half], hh[half:]
    yr = xr * hr - xi * hi
    yi = xr * hi + xi * hr
    y = jnp.concatenate([yr, yi], axis=0).astype(BF16)
    o_ref[0] = jnp.dot(mi_ref[...], y, preferred_element_type=F32).astype(o_ref.dtype)


def _conv_stage_b(a_arr, mf, mi, hspec, n1):
    _, rows, c = a_arr.shape
    a2 = a_arr.reshape(DFT_MINOR, rows * c)
    return pl.pallas_call(
        _conv_b_kernel,
        grid=(n1,),
        in_specs=[pl.BlockSpec((DFT_MINOR, c), lambda k1: (0, k1)),
                  pl.BlockSpec((DFT_MINOR, c), lambda k1: (0, n1 + k1)),
                  pl.BlockSpec((2 * DFT_MINOR, 2 * DFT_MINOR), lambda k1: (0, 0)),
                  pl.BlockSpec((2 * DFT_MINOR, 2 * DFT_MINOR), lambda k1: (0, 0)),
                  pl.BlockSpec((1, 2 * DFT_MINOR, c), lambda k1: (k1, 0, 0))],
        out_specs=pl.BlockSpec((1, 2 * DFT_MINOR, c), lambda k1: (k1, 0, 0)),
        out_shape=jax.ShapeDtypeStruct((n1, 2 * DFT_MINOR, c), BF16),
        compiler_params=_params(("parallel",)),
        name="conv_stage_b",
    )(a2, a2, mf, mi, hspec)


def _conv_out_kernel(re_ref, im_ref, tab_ref, o_ref):
    rhs = jnp.concatenate([re_ref[...], im_ref[...]], axis=0)
    y = jnp.dot(tab_ref[0], rhs, preferred_element_type=F32)
    o_ref[...] = y.reshape(o_ref.shape).astype(o_ref.dtype)


def _conv_stage_out(b_arr, tab, batch, n1):
    _, rows, c = b_arr.shape
    b2 = b_arr.reshape(n1, rows * c)
    r = n1 // 2
    return pl.pallas_call(
        _conv_out_kernel,
        grid=(DFT_MINOR,),
        in_specs=[pl.BlockSpec((n1, c), lambda t2: (0, t2)),
                  pl.BlockSpec((n1, c), lambda t2: (0, DFT_MINOR + t2)),
                  pl.BlockSpec((1, batch * r, 2 * n1), lambda t2: (t2, 0, 0))],
        out_specs=pl.BlockSpec((batch, r, c), lambda t2: (0, 0, t2)),
        out_shape=jax.ShapeDtypeStruct((batch, r, DFT_MINOR * c), BF16),
        compiler_params=_params(("parallel",)),
        name="conv_stage_out",
    )(b2, b2, tab)


def _dft_tables(n1, n_sig_rows):
    n = n1 * DFT_MINOR
    k1 = np.arange(n1, dtype=np.int64)[None, :, None]
    t2 = np.arange(DFT_MINOR, dtype=np.int64)[:, None, None]

    def twiddled(n_t1):
        t1 = np.arange(n_t1, dtype=np.int64)[None, None, :]
        m = (k1 * (DFT_MINOR * t1 + t2)) % n
        ang = m.astype(np.float64) * (-2.0 * math.pi / n)
        return np.cos(ang).astype(np.float32), np.sin(ang).astype(np.float32)

    pr, pi = twiddled(n_sig_rows)
    tab_sig = np.concatenate([np.concatenate([pr, -pi], axis=2), np.concatenate([pi, pr], axis=2)], axis=1)
    prt, pit = np.swapaxes(pr, 1, 2), np.swapaxes(pi, 1, 2)
    tab_out = np.concatenate([np.concatenate([prt, pit], axis=2), np.concatenate([-pit, prt], axis=2)], axis=1)
    fr, fi = twiddled(n1)
    tab_filt = np.concatenate([fr, fi], axis=1)
    j = np.arange(DFT_MINOR)
    ang = -2.0 * np.pi * ((j[:, None] * j[None, :]) % DFT_MINOR) / DFT_MINOR
    cr, ci = np.cos(ang), np.sin(ang)
    mf = np.block([[cr, -ci], [ci, cr]])
    mi = np.block([[cr, ci], [-ci, cr]])
    return tuple(jnp.asarray(a, BF16) for a in (tab_sig, tab_out, tab_filt, mf, mi))


def _merge_a_kernel(a_ref, yc_ref, z_ref, x0_ref, d_ref, ga_ref, gh_ref, bga_ref, bgh_ref, wa_ref, wh_ref, o_ref, hy_scr):
    @pl.when(pl.program_id(1) == 0)
    def _():
        z = z_ref[...].astype(F32)
        hy = x0_ref[...].astype(F32) * (yc_ref[...].astype(F32) + z * d_ref[...])
        hy_scr[...] = hy.astype(BF16)

    ya = jnp.dot(a_ref[...], wa_ref[...], preferred_element_type=F32)
    yh = jnp.dot(hy_scr[...], wh_ref[...], preferred_element_type=F32)
    ga = jax.nn.sigmoid(ga_ref[...].astype(F32) + bga_ref[...])
    gh = jax.nn.sigmoid(gh_ref[...].astype(F32) + bgh_ref[...])
    o_ref[...] = (ga * ya + gh * yh).astype(BF16)


def _merge_a(attn, yconv, z, x0c, d_bias, proj, b_gate2, wa, wh, gate_col0, tm=512, tn=512):
    t, aw = attn.shape
    hw = yconv.shape[1]
    d = wa.shape[1]
    assert gate_col0 % tn == 0 and d % tn == 0
    gb = gate_col0 // tn
    nj = d // tn
    hy_spec = pl.BlockSpec((tm, hw), lambda i, j: (i, 0))
    return pl.pallas_call(
        _merge_a_kernel,
        grid=(t // tm, nj),
        in_specs=[pl.BlockSpec((tm, aw), lambda i, j: (i, 0)),
                  hy_spec, hy_spec, hy_spec,
                  pl.BlockSpec((1, hw), lambda i, j: (0, 0)),
                  pl.BlockSpec((tm, tn), lambda i, j: (i, gb + j)),
                  pl.BlockSpec((tm, tn), lambda i, j: (i, gb + nj + j)),
                  pl.BlockSpec((1, tn), lambda i, j: (0, j)),
                  pl.BlockSpec((1, tn), lambda i, j: (0, nj + j)),
                  pl.BlockSpec((aw, tn), lambda i, j: (0, j)),
                  pl.BlockSpec((hw, tn), lambda i, j: (0, j))],
        out_specs=pl.BlockSpec((tm, tn), lambda i, j: (i, j)),
        out_shape=jax.ShapeDtypeStruct((t, d), BF16),
        scratch_shapes=[pltpu.VMEM((tm, hw), BF16)],
        compiler_params=_params(("parallel", "arbitrary")),
        name="merge_gated",
    )(attn, yconv, z, x0c, d_bias, proj, proj, b_gate2, b_gate2, wa, wh)


def _merge_b_kernel(u_ref, x_ref, gi_ref, bi_ref, wo_ref, g1_ref, b1_ref, wr_ref, br_ref, h_ref, lg_ref):
    mixed = jnp.dot(u_ref[...], wo_ref[...], preferred_element_type=F32)
    h0 = _layer_norm(x_ref[...], gi_ref[...], bi_ref[...])
    h1 = _layer_norm(DEEPNORM_ALPHA * h0 + mixed, g1_ref[...], b1_ref[...])
    h_ref[...] = h1
    wr = wr_ref[...]
    h_hi = h1.astype(BF16)
    h_lo = (h1 - h_hi.astype(F32)).astype(BF16)
    w_hi = wr.astype(BF16)
    w_lo = (wr - w_hi.astype(F32)).astype(BF16)
    lg = jnp.dot(h_hi, w_hi, preferred_element_type=F32)
    lg += jnp.dot(h_hi, w_lo, preferred_element_type=F32)
    lg += jnp.dot(h_lo, w_hi, preferred_element_type=F32)
    lg_ref[...] = lg + br_ref[...]


def _merge_b(u, x2, gi, bi, wo, g1, b1, wr, br, tm=512):
    t, d = x2.shape
    const = lambda i: (0, 0)
    return pl.pallas_call(
        _merge_b_kernel,
        grid=(t // tm,),
        in_specs=[pl.BlockSpec((tm, d), lambda i: (i, 0)),
                  pl.BlockSpec((tm, d), lambda i: (i, 0)),
                  pl.BlockSpec((1, d), const), pl.BlockSpec((1, d), const),
                  pl.BlockSpec((d, d), const),
                  pl.BlockSpec((1, d), const), pl.BlockSpec((1, d), const),
                  pl.BlockSpec((d, LANES), const), pl.BlockSpec((1, LANES), const)],
        out_specs=[pl.BlockSpec((tm, d), lambda i: (i, 0)), pl.BlockSpec((tm, LANES), lambda i: (i, 0))],
        out_shape=[jax.ShapeDtypeStruct((t, d), F32), jax.ShapeDtypeStruct((t, LANES), F32)],
        compiler_params=_params(("parallel",)),
        name="merge_out_ln1",
    )(u, x2, gi, bi, wo, g1, b1, wr, br)


def _route_kernel(lg_ref, tri_ref, id_ref, gt_ref, cnt_ref, pre_scr):
    x = lg_ref[...]
    lane = lax.broadcasted_iota(jnp.int32, x.shape, 1).astype(F32)
    big = jnp.float32(1 << 20)
    neg = -jnp.inf
    cmask = lane < N_GROUPS
    cm = jnp.max(jnp.where(cmask, x, neg), axis=-1, keepdims=True)
    grp = jnp.min(jnp.where(cmask & (x == cm), lane, big), axis=-1, keepdims=True)
    csum = jnp.sum(jnp.where(cmask, jnp.exp(x - cm), 0.0), axis=-1, keepdims=True)
    p_grp = 1.0 / csum
    lo = N_GROUPS + grp * EXPERTS_PER_GROUP
    fmask = (lane >= lo) & (lane < lo + EXPERTS_PER_GROUP)
    f1 = jnp.max(jnp.where(fmask, x, neg), axis=-1, keepdims=True)
    i1 = jnp.min(jnp.where(fmask & (x == f1), lane, big), axis=-1, keepdims=True)
    mask2 = fmask & (lane != i1)
    f2 = jnp.max(jnp.where(mask2, x, neg), axis=-1, keepdims=True)
    i2 = jnp.min(jnp.where(mask2 & (x == f2), lane, big), axis=-1, keepdims=True)
    e2 = jnp.exp(f2 - f1)
    g1 = p_grp / (1.0 + e2)
    g2 = p_grp * e2 / (1.0 + e2)
    gt_ref[...] = jnp.where(lane == 0, g1, jnp.where(lane == 1, g2, 0.0))

    @pl.when(pl.program_id(0) == 0)
    def _():
        cnt_ref[...] = jnp.zeros_like(cnt_ref)

    sel1 = lane == i1
    sel2 = lane == i2
    chosen = jnp.where(sel1 | sel2, 1.0, 0.0)
    sub = tri_ref.shape[0]
    carry = cnt_ref[...]
    for s in range(x.shape[0] // sub):
        cs = chosen[s * sub:(s + 1) * sub]
        pre_scr[s * sub:(s + 1) * sub, :] = jnp.dot(tri_ref[...], cs.astype(BF16), preferred_element_type=F32) + carry
        carry = carry + jnp.sum(cs, axis=0, keepdims=True)
    cnt_ref[...] = carry
    before = pre_scr[...]
    r1 = jnp.sum(jnp.where(sel1, before, 0.0), axis=-1, keepdims=True)
    r2 = jnp.sum(jnp.where(sel2, before, 0.0), axis=-1, keepdims=True)
    ids = jnp.where(lane == 0, i1 - N_GROUPS, jnp.where(lane == 1, i2 - N_GROUPS,
                    jnp.where(lane == 2, r1, jnp.where(lane == 3, r2, 0.0))))
    id_ref[...] = ids.astype(jnp.int32)


def _route(logits, tm=2048, sub=256):
    t = logits.shape[0]
    spec = pl.BlockSpec((tm, LANES), lambda i: (i, 0))
    tri = jnp.asarray(np.tril(np.ones((sub, sub), np.float32), -1), BF16)
    return pl.pallas_call(
        _route_kernel,
        grid=(t // tm,),
        in_specs=[spec, pl.BlockSpec((sub, sub), lambda i: (0, 0))],
        out_specs=[spec, spec, pl.BlockSpec((1, LANES), lambda i: (0, 0))],
        out_shape=[jax.ShapeDtypeStruct((t, LANES), jnp.int32), jax.ShapeDtypeStruct((t, LANES), F32),
                   jax.ShapeDtypeStruct((1, LANES), F32)],
        scratch_shapes=[pltpu.VMEM((tm, LANES), F32)],
        compiler_params=_params(("arbitrary",)),
        name="route_topk",
    )(logits, tri)


def _invert_kernel(d0_ref, d1_ref, lo_ref, hi_ref, o_ref):
    n_tok = d0_ref.shape[0]

    def zero(i, c):
        o_ref[i] = 0
        return c

    for e in range(lo_ref.shape[0]):
        lax.fori_loop(lo_ref[e], hi_ref[e], zero, 0)

    def put(t, c):
        o_ref[d0_ref[t]] = t
        o_ref[d1_ref[t]] = t
        return c
    lax.fori_loop(0, n_tok, put, 0, unroll=8)


def _invert(dest0, dest1, pad_lo, pad_hi, n_rows):
    return pl.pallas_call(
        _invert_kernel,
        grid_spec=pltpu.PrefetchScalarGridSpec(
            num_scalar_prefetch=4, grid=(1,), in_specs=[],
            out_specs=pl.BlockSpec(memory_space=pltpu.SMEM)),
        out_shape=jax.ShapeDtypeStruct((n_rows,), jnp.int32),
        compiler_params=_params(("arbitrary",)),
        name="dispatch_invert",
    )(dest0, dest1, pad_lo, pad_hi)


def _row_gather(src_hbm, idx_ref, base, dst, sem, n_rows):
    for r in range(n_rows):
        tok = idx_ref[base + r]
        pltpu.make_async_copy(src_hbm.at[pl.ds(tok, 1)], dst.at[pl.ds(r, 1)], sem).start()


def _row_gather_wait(src_hbm, dst, sem, n_rows):
    for r in range(n_rows):
        pltpu.make_async_copy(src_hbm.at[pl.ds(0, 1)], dst.at[pl.ds(r, 1)], sem).wait()


def _moe_kernel(be_ref, nb_ref, tok_ref, h_hbm, wg_ref, wu_ref, wd_ref, o_ref, xbuf, sem):
    i = pl.program_id(0)
    n_used = nb_ref[0]
    slot = i % 2
    rows = xbuf.shape[1]

    @pl.when(i == 0)
    def _():
        _row_gather(h_hbm, tok_ref, 0, xbuf.at[0], sem.at[0], rows)

    @pl.when(i < n_used)
    def _():
        _row_gather_wait(h_hbm, xbuf.at[slot], sem.at[slot], rows)
        x = xbuf[slot].astype(BF16)
        _row_gather(h_hbm, tok_ref, (i + 1) * rows, xbuf.at[1 - slot], sem.at[1 - slot], rows)
        g = jnp.dot(x, wg_ref[0], preferred_element_type=F32)
        u = jnp.dot(x, wu_ref[0], preferred_element_type=F32)
        hdn = (g * jax.nn.sigmoid(g) * u).astype(BF16)
        o_ref[...] = jnp.dot(hdn, wd_ref[0], preferred_element_type=F32)

    @pl.when(i == n_used)
    def _():
        _row_gather_wait(h_hbm, xbuf.at[slot], sem.at[slot], rows)

    @pl.when(i >= n_used)
    def _():
        o_ref[...] = jnp.zeros_like(o_ref)


def _moe_ffn(blk_expert, n_used, row_tok, h1, wg, wu, wd, rows=MOE_ROWS):
    t, d = h1.shape
    f = wg.shape[2]
    n_blk = blk_expert.shape[0]
    assert row_tok.shape[0] == (n_blk + 1) * rows
    grid_spec = pltpu.PrefetchScalarGridSpec(
        num_scalar_prefetch=3,
        grid=(n_blk,),
        in_specs=[pl.BlockSpec(memory_space=pl.ANY),
                  pl.BlockSpec((1, d, f), lambda i, be, nb, tk: (be[i], 0, 0)),
                  pl.BlockSpec((1, d, f), lambda i, be, nb, tk: (be[i], 0, 0)),
                  pl.BlockSpec((1, f, d), lambda i, be, nb, tk: (be[i], 0, 0))],
        out_specs=pl.BlockSpec((rows, d), lambda i, be, nb, tk: (i, 0)),
        scratch_shapes=[pltpu.VMEM((2, rows, d), F32), pltpu.SemaphoreType.DMA((2,))],
    )
    return pl.pallas_call(
        _moe_kernel,
        grid_spec=grid_spec,
        out_shape=jax.ShapeDtypeStruct((n_blk * rows, d), F32),
        compiler_params=_params(("arbitrary",)),
        name="moe_ffn",
    )(blk_expert, n_used, row_tok, h1, wg, wu, wd)


def _combine_kernel(p0_ref, p1_ref, ys_hbm, h_ref, gt_ref, g2_ref, b2_ref, o_ref, ybuf, sem):
    i = pl.program_id(0)
    n = pl.num_programs(0)
    slot = i % 2
    tm = h_ref.shape[0]

    def issue(step, s):
        _row_gather(ys_hbm, p0_ref, step * tm, ybuf.at[s, 0], sem.at[s], tm)
        _row_gather(ys_hbm, p1_ref, step * tm, ybuf.at[s, 1], sem.at[s], tm)

    def drain(s):
        _row_gather_wait(ys_hbm, ybuf.at[s, 0], sem.at[s], tm)
        _row_gather_wait(ys_hbm, ybuf.at[s, 1], sem.at[s], tm)

    @pl.when(i == 0)
    def _():
        issue(0, 0)

    drain(slot)
    gt = gt_ref[...]
    moe = gt[:, 0:1] * ybuf[slot, 0] + gt[:, 1:2] * ybuf[slot, 1]
    issue(i + 1, 1 - slot)
    o_ref[...] = _layer_norm(DEEPNORM_ALPHA * h_ref[...] + moe, g2_ref[...], b2_ref[...])

    @pl.when(i == n - 1)
    def _():
        drain(1 - slot)


def _combine(pos0, pos1, ys, h1, gates, g2, b2, tm=256):
    t, d = h1.shape
    assert pos0.shape[0] == t + tm and pos1.shape[0] == t + tm
    const = lambda i, a, b: (0, 0)
    grid_spec = pltpu.PrefetchScalarGridSpec(
        num_scalar_prefetch=2,
        grid=(t // tm,),
        in_specs=[pl.BlockSpec(memory_space=pl.ANY),
                  pl.BlockSpec((tm, d), lambda i, a, b: (i, 0)),
                  pl.BlockSpec((tm, LANES), lambda i, a, b: (i, 0)),
                  pl.BlockSpec((1, d), const), pl.BlockSpec((1, d), const)],
        out_specs=pl.BlockSpec((tm, d), lambda i, a, b: (i, 0)),
        scratch_shapes=[pltpu.VMEM((2, 2, tm, d), F32), pltpu.SemaphoreType.DMA((2,))],
    )
    return pl.pallas_call(
        _combine_kernel,
        grid_spec=grid_spec,
        out_shape=jax.ShapeDtypeStruct((t, d), F32),
        compiler_params=_params(("arbitrary",)),
        name="moe_combine_ln2",
    )(pos0, pos1, ys, h1, gates, g2, b2)


def _rope_tables(seq):
    rows = seq // GRID_W
    row_idx = jnp.repeat(jnp.arange(rows, dtype=jnp.int32), GRID_W).astype(F32)
    col_idx = jnp.tile(jnp.arange(GRID_W, dtype=jnp.int32), rows).astype(F32)
    half = HEAD_DIM // 2
    inv_freq = ROPE_THETA ** (-jnp.arange(0, half, 2, dtype=F32) / half)
    ang_r = row_idx[:, None] * inv_freq[None, :]
    ang_c = col_idx[:, None] * inv_freq[None, :]
    cr, sr, cc, sc = jnp.cos(ang_r), jnp.sin(ang_r), jnp.cos(ang_c), jnp.sin(ang_c)
    cs = jnp.concatenate([cr, cc, cr, cc], axis=-1)
    sn = jnp.concatenate([-sr, -sc, sr, sc], axis=-1)
    return cs, sn


def _pair_split(v):
    lead = v.shape[:-1]
    q4 = v.reshape(lead + (-1, 2, 2, HEAD_DIM // 4))
    return jnp.swapaxes(q4, -3, -2).reshape(v.shape)


def _filter_features(seq):
    n1 = 2 * seq // DFT_MINOR
    n = (DFT_MINOR * np.arange(n1)[None, :] + np.arange(DFT_MINOR)[:, None]).astype(np.int64)
    j = np.where(n <= seq, n, 2 * seq - n)
    j = np.where(n == seq, 0, j).astype(np.float64)
    t = (j / (seq - 1))[..., None]
    w = (2.0 * math.pi * j / seq)[..., None]
    bands = np.linspace(1e-4, FILTER_BANDS - 1, FILTER_BANDS, dtype=np.float32).astype(np.float64)
    feats = np.zeros(n.shape + (LANES,), np.float32)
    feats[..., 0:1] = t
    feats[..., 1:1 + FILTER_BANDS] = np.cos(bands * w)
    feats[..., 1 + FILTER_BANDS:FILTER_EMB] = -np.sin(bands * w)
    feats[..., LANES - 1] = (n != seq)
    return jnp.asarray(feats)


def _dispatch_plan(ids, counts, n_tok, rows, tail):
    counts = counts.astype(jnp.int32)
    padded = (counts + rows - 1) // rows * rows
    pad_end = jnp.cumsum(padded)
    pad_start = pad_end - padded
    n_blk = (n_tok * TOP_K + N_EXPERTS * rows) // rows + 1
    blk_start = jnp.arange(n_blk, dtype=jnp.int32) * rows
    blk_expert = jnp.minimum(jnp.sum((pad_end[None, :] <= blk_start[:, None]).astype(jnp.int32), axis=1),
                             N_EXPERTS - 1)
    n_used = (pad_end[-1] // rows).astype(jnp.int32).reshape(1)
    experts = jnp.arange(N_EXPERTS, dtype=jnp.int32)[None, :]

    def dest(e, rank):
        start = jnp.sum(jnp.where(e[:, None] == experts, pad_start[None, :], 0), axis=1)
        return jnp.pad(start + rank, (0, tail))

    pos0 = dest(ids[:, 0], ids[:, 2])
    pos1 = dest(ids[:, 1], ids[:, 3])
    n_map = (n_blk + 1) * rows
    pad_lo = pad_start + counts
    pad_hi = pad_end.at[N_EXPERTS - 1].set(n_map)
    row_tok = _invert(pos0[:n_tok], pos1[:n_tok], pad_lo, pad_hi, n_map)
    return blk_expert, n_used, row_tok, pos0, pos1


def kernel(x, ln_in_g, ln_in_b, w_in, b_gate, q_norm_g, k_norm_g, hy_conv_w, hy_conv_b, filt_w1, filt_b1, filt_f1, filt_w2, filt_b2, filt_f2, filt_w3, hy_bias_d, w_attn_o, w_hy_o, w_out, ln1_g, ln1_b, w_route_grp, b_route_grp, w_route_exp, b_route_exp, w_exp_gate, w_exp_up, w_exp_down, ln2_g, ln2_b):
    batch, seq, d = x.shape
    assert batch == 2, "the long convolution packs exactly two batch rows as one complex signal"
    t = batch * seq
    hw = hy_bias_d.shape[1]
    l = 0
    x2 = x.reshape(t, d)
    row = lambda v: v.reshape(1, -1)

    cs, sn = _rope_tables(seq)
    qg = row(q_norm_g[l]) * (HEAD_DIM ** -0.5 * math.log2(math.e))
    qg = _pair_split(qg)
    kg = _pair_split(row(k_norm_g[l]))
    n_qk = ATTN_WIDTH + KV_WIDTH
    w_bf = jnp.concatenate([_pair_split(w_in[l][:, :n_qk]).astype(BF16), w_in[l][:, n_qk:].astype(BF16)], axis=1)
    proj = _ln_inproj(x2, row(ln_in_g), row(ln_in_b), w_bf, qg, kg, cs, sn, seq)

    attn = _attention(proj, batch, seq)

    hy_col0 = ATTN_WIDTH + 2 * KV_WIDTH
    z, x0c = _hy_pre(proj, hy_conv_w[l], row(hy_conv_b[l]), batch, seq, hw, hy_col0)
    n1 = 2 * seq // DFT_MINOR
    tab_sig, tab_out, tab_filt, mf, mi = _dft_tables(n1, n1 // 2)
    feats = _filter_features(seq)
    w1p = jnp.pad(filt_w1[l], ((0, LANES - FILTER_EMB), (0, 0)))
    min_decay = math.log(DECAY_TARGET) / SLOW_DECAY_PCT
    max_decay = math.log(DECAY_TARGET) / FAST_DECAY_PCT
    deltas = jnp.linspace(min_decay, max_decay, hw, dtype=F32)[None, :]
    two_sided, sumsq = _filter_time(feats, w1p, row(filt_b1[l]), row(filt_f1[l]), filt_w2[l], row(filt_b2[l]),
                                    row(filt_f2[l]), filt_w3[l], deltas)
    fa = _dft_stage_a(two_sided, tab_filt, 2 * n1, t2_major=True)
    hspec = _filter_stage_b(fa, mf, sumsq, n1)
    za = _dft_stage_a(z.reshape(batch, n1 // 2, DFT_MINOR * hw), tab_sig, 2 * n1)
    zb = _conv_stage_b(za, mf, mi, hspec, n1)
    yconv = _conv_stage_out(zb, tab_out, batch, n1).reshape(t, hw)

    gate_col0 = hy_col0 + 3 * hw
    u = _merge_a(attn, yconv, z, x0c, row(hy_bias_d[l]), proj, row(b_gate[l]), w_attn_o[l].astype(BF16),
                 w_hy_o[l].astype(BF16), gate_col0)
    n_r = N_GROUPS + N_EXPERTS
    wr = jnp.pad(jnp.concatenate([w_route_grp[l], w_route_exp[l]], axis=1), ((0, 0), (0, LANES - n_r)))
    br = jnp.pad(jnp.concatenate([b_route_grp[l], b_route_exp[l]]), (0, LANES - n_r)).reshape(1, LANES)
    h1, logits = _merge_b(u, x2, row(ln_in_g), row(ln_in_b), w_out[l].astype(BF16), row(ln1_g[l]), row(ln1_b[l]),
                          wr, br)

    ids, gates, cnt = _route(logits)
    counts = cnt[0, N_GROUPS:N_GROUPS + N_EXPERTS]
    combine_tile = 256
    blk_expert, n_used, row_tok, pos0, pos1 = _dispatch_plan(ids, counts, t, MOE_ROWS, combine_tile)
    ys = _moe_ffn(blk_expert, n_used, row_tok, h1, w_exp_gate[l].astype(BF16), w_exp_up[l].astype(BF16),
                  w_exp_down[l].astype(BF16))
    out = _combine(pos0, pos1, ys, h1, gates, row(ln2_g[l]), row(ln2_b[l]), tm=combine_tile)
    return out.reshape(batch, seq, d)
```

```python
import functools
import math

import jax
import jax.numpy as jnp
import numpy as np
from jax import lax
from jax.experimental import pallas as pl
from jax.experimental.pallas import tpu as pltpu

F32 = jnp.float32
BF16 = jnp.bfloat16

GRID_W = 64
N_Q_HEADS = 8
N_KV_HEADS = 2
HEAD_DIM = 128
Q_GROUP = N_Q_HEADS // N_KV_HEADS
ATTN_WIDTH = N_Q_HEADS * HEAD_DIM
KV_WIDTH = N_KV_HEADS * HEAD_DIM
ROPE_THETA = 10000.0
QK_EPS = 1e-6
SHORT_CONV = 3
FILTER_BANDS = 16
FILTER_EMB = 1 + 2 * FILTER_BANDS
DECAY_TARGET = 1e-2
FAST_DECAY_PCT = 0.3
SLOW_DECAY_PCT = 1.5
FILTER_EPS = 1e-6
N_GROUPS = 4
EXPERTS_PER_GROUP = 8
N_EXPERTS = N_GROUPS * EXPERTS_PER_GROUP
TOP_K = 2
LN_EPS = 1e-5
DEPTH = 1
DEEPNORM_ALPHA = (2 * DEPTH) ** 0.25

LANES = 128
V7X_VMEM_LIMIT = 56 * 1024 * 1024
DFT_MINOR = 128

MOE_ROWS = 256


def _params(sem, vmem=V7X_VMEM_LIMIT):
    return pltpu.CompilerParams(dimension_semantics=sem, vmem_limit_bytes=vmem)


def _layer_norm(x, g, b):
    mu = jnp.mean(x, axis=-1, keepdims=True)
    xc = x - mu
    var = jnp.mean(xc * xc, axis=-1, keepdims=True)
    return xc * lax.rsqrt(var + LN_EPS) * g + b


def _norm_rope(acc, gain, cs, sn):
    ms = jnp.mean(acc * acc, axis=-1, keepdims=True)
    xn = acc * lax.rsqrt(ms + QK_EPS) * gain
    return xn * cs + pltpu.roll(xn, HEAD_DIM // 2, 1) * sn


def _ln_inproj_kernel(x_ref, g_ref, b_ref, w_ref, qg_ref, kg_ref, cs_ref, sn_ref, o_ref, h_scr, *, tn):
    j = pl.program_id(1)
    nchunk = tn // HEAD_DIM
    n_q = ATTN_WIDTH // HEAD_DIM
    n_qk = n_q + N_KV_HEADS

    @pl.when(j == 0)
    def _():
        h = _layer_norm(x_ref[...], g_ref[...], b_ref[...])
        h_scr[...] = h.astype(BF16)

    acc = jnp.dot(h_scr[...], w_ref[...], preferred_element_type=F32)

    def store(first_plain_chunk_fn):
        for c in range(nchunk):
            sl = slice(c * HEAD_DIM, (c + 1) * HEAD_DIM)
            kind = first_plain_chunk_fn(c)
            if kind == "q":
                o_ref[:, sl] = _norm_rope(acc[:, sl], qg_ref[...], cs_ref[...], sn_ref[...]).astype(BF16)
            elif kind == "k":
                o_ref[:, sl] = _norm_rope(acc[:, sl], kg_ref[...], cs_ref[...], sn_ref[...]).astype(BF16)
            else:
                o_ref[:, sl] = acc[:, sl].astype(BF16)

    n_q_tiles = n_q // nchunk
    assert n_q % nchunk == 0 and N_KV_HEADS <= nchunk

    @pl.when(j < n_q_tiles)
    def _():
        store(lambda c: "q")

    @pl.when(j == n_q_tiles)
    def _():
        store(lambda c: "k" if c < N_KV_HEADS else "p")

    @pl.when(j > n_q_tiles)
    def _():
        store(lambda c: "p")


def _ln_inproj(x2, g, b, w_bf, qg, kg, cs, sn, seq, tm=1024, tn=512):
    t, d = x2.shape
    n = w_bf.shape[1]
    assert t % tm == 0 and n % tn == 0 and seq % tm == 0
    pos_blocks = seq // tm
    return pl.pallas_call(
        functools.partial(_ln_inproj_kernel, tn=tn),
        grid=(t // tm, n // tn),
        in_specs=[
            pl.BlockSpec((tm, d), lambda i, j: (i, 0)),
            pl.BlockSpec((1, d), lambda i, j: (0, 0)),
            pl.BlockSpec((1, d), lambda i, j: (0, 0)),
            pl.BlockSpec((d, tn), lambda i, j: (0, j)),
            pl.BlockSpec((1, HEAD_DIM), lambda i, j: (0, 0)),
            pl.BlockSpec((1, HEAD_DIM), lambda i, j: (0, 0)),
            pl.BlockSpec((tm, HEAD_DIM), lambda i, j: (i % pos_blocks, 0)),
            pl.BlockSpec((tm, HEAD_DIM), lambda i, j: (i % pos_blocks, 0)),
        ],
        out_specs=pl.BlockSpec((tm, tn), lambda i, j: (i, j)),
        out_shape=jax.ShapeDtypeStruct((t, n), BF16),
        scratch_shapes=[pltpu.VMEM((tm, d), BF16)],
        compiler_params=_params(("parallel", "arbitrary")),
        name="ln_inproj",
    )(x2, g, b, w_bf, qg, kg, cs, sn)


def _attn_kernel(q_ref, k_ref, v_ref, o_ref, vt_scr, *, tq, tk):
    seq = k_ref.shape[0]
    n_chunks = seq // tk
    m_cols = Q_GROUP * tq

    @pl.when(pl.program_id(2) == 0)
    def _():
        for c in range(n_chunks):
            vt_scr[c] = v_ref[c * tk:(c + 1) * tk, :].T

    qt = jnp.concatenate([q_ref[:, h * HEAD_DIM:(h + 1) * HEAD_DIM].T for h in range(Q_GROUP)], axis=1)

    def scores(c):
        start = pl.multiple_of(c * tk, tk)
        return jnp.dot(k_ref[pl.ds(start, tk), :], qt, preferred_element_type=F32)

    def update(c, s, m, l, acc):
        m_new = jnp.maximum(m, jnp.max(s, axis=0, keepdims=True))
        p = jnp.exp2(s - m_new)
        alpha = jnp.exp2(m - m_new)
        l_new = alpha * l + jnp.sum(p, axis=0, keepdims=True)
        acc_new = alpha * acc + jnp.dot(vt_scr[c], p.astype(BF16), preferred_element_type=F32)
        return m_new, l_new, acc_new

    def body(c, carry):
        m, l, acc, s = carry
        s_next = scores(jnp.minimum(c + 1, n_chunks - 1))
        m, l, acc = update(c, s, m, l, acc)
        return m, l, acc, s_next

    m0 = jnp.full((1, m_cols), -jnp.inf, F32)
    l0 = jnp.zeros((1, m_cols), F32)
    a0 = jnp.zeros((HEAD_DIM, m_cols), F32)
    _, l, acc, _ = lax.fori_loop(0, n_chunks, body, (m0, l0, a0, scores(0)), unroll=8)
    out = acc / l
    for h in range(Q_GROUP):
        o_ref[:, h * HEAD_DIM:(h + 1) * HEAD_DIM] = out[:, h * tq:(h + 1) * tq].T.astype(BF16)


def _attention(proj, batch, seq, tq=128, tk=512):
    t = proj.shape[0]
    nq = seq // tq
    gw = Q_GROUP * HEAD_DIM
    k_col0 = ATTN_WIDTH // HEAD_DIM
    v_col0 = (ATTN_WIDTH + KV_WIDTH) // HEAD_DIM
    return pl.pallas_call(
        functools.partial(_attn_kernel, tq=tq, tk=tk),
        grid=(batch, N_KV_HEADS, nq),
        in_specs=[
            pl.BlockSpec((tq, gw), lambda b, g, i: (b * nq + i, g)),
            pl.BlockSpec((seq, HEAD_DIM), lambda b, g, i: (b, k_col0 + g)),
            pl.BlockSpec((seq, HEAD_DIM), lambda b, g, i: (b, v_col0 + g)),
        ],
        out_specs=pl.BlockSpec((tq, gw), lambda b, g, i: (b * nq + i, g)),
        out_shape=jax.ShapeDtypeStruct((t, ATTN_WIDTH), BF16),
        scratch_shapes=[pltpu.VMEM((seq // tk, HEAD_DIM, tk), BF16)],
        compiler_params=_params(("parallel", "parallel", "arbitrary")),
        name="attention",
    )(proj, proj, proj)


def _hy_pre_kernel(x0_ref, x1_ref, hv_ref,
                   p0_ref, p1_ref, pv_ref, n0_ref, n1_ref, nv_ref,
                   w0_ref, w1_ref, wv_ref, b0_ref, b1_ref, bv_ref,
                   z_ref, x0c_ref):
    r = pl.program_id(2)
    last = pl.num_programs(2) - 1
    ts = x0_ref.shape[0]
    row = lax.broadcasted_iota(jnp.int32, x0_ref.shape, 0)

    def conv(x_ref, p_ref, n_ref, w_ref, b_ref):
        x = x_ref[...].astype(F32)
        prev_row = jnp.where(r == 0, 0.0, p_ref[7:8, :].astype(F32))
        next_row = jnp.where(r == last, 0.0, n_ref[0:1, :].astype(F32))
        up = jnp.where(row == 0, prev_row, pltpu.roll(x, 1, 0))
        dn = jnp.where(row == ts - 1, next_row, pltpu.roll(x, ts - 1, 0))
        w = w_ref[...]
        return b_ref[...] + up * w[0:1, :] + x * w[1:2, :] + dn * w[2:3, :]

    x0c = conv(x0_ref, p0_ref, n0_ref, w0_ref, b0_ref)
    x1c = conv(x1_ref, p1_ref, n1_ref, w1_ref, b1_ref)
    hvc = conv(hv_ref, pv_ref, nv_ref, wv_ref, bv_ref)
    z_ref[...] = (hvc * x1c).astype(BF16)
    x0c_ref[...] = x0c.astype(BF16)


def _hy_pre(proj, conv_w, conv_b, batch, seq, hw, col0, ts=512, tc=256):
    t = proj.shape[0]
    nr = seq // ts
    nct = hw // tc
    cb0 = col0 // tc
    halo = 8
    hb = ts // halo

    def main(off):
        return pl.BlockSpec((ts, tc), lambda b, c, r: (b * nr + r, cb0 + off * nct + c))

    def prev(off):
        return pl.BlockSpec((halo, tc), lambda b, c, r: (jnp.maximum((b * nr + r) * hb - 1, 0), cb0 + off * nct + c))

    def nxt(off):
        return pl.BlockSpec((halo, tc), lambda b, c, r: (jnp.minimum((b * nr + r + 1) * hb, t // halo - 1),
                                                       cb0 + off * nct + c))

    def wspec(off):
        return pl.BlockSpec((SHORT_CONV, tc), lambda b, c, r: (0, off * nct + c))

    def bspec(off):
        return pl.BlockSpec((1, tc), lambda b, c, r: (0, off * nct + c))

    out_spec = pl.BlockSpec((ts, tc), lambda b, c, r: (b * nr + r, c))
    return pl.pallas_call(
        _hy_pre_kernel,
        grid=(batch, nct, nr),
        in_specs=[main(0), main(1), main(2), prev(0), prev(1), prev(2), nxt(0), nxt(1), nxt(2),
                  wspec(0), wspec(1), wspec(2), bspec(0), bspec(1), bspec(2)],
        out_specs=[out_spec, out_spec],
        out_shape=[jax.ShapeDtypeStruct((t, hw), BF16), jax.ShapeDtypeStruct((t, hw), BF16)],
        compiler_params=_params(("parallel", "parallel", "parallel")),
        name="hy_pre",
    )(proj, proj, proj, proj, proj, proj, proj, proj, proj,
      conv_w, conv_w, conv_w, conv_b, conv_b, conv_b)


def _dot3(a, b):
    a_hi = a.astype(BF16)
    a_lo = (a - a_hi.astype(F32)).astype(BF16)
    b_hi = b.astype(BF16)
    b_lo = (b - b_hi.astype(F32)).astype(BF16)
    out = jnp.dot(a_hi, b_hi, preferred_element_type=F32)
    out += jnp.dot(a_hi, b_lo, preferred_element_type=F32)
    out += jnp.dot(a_lo, b_hi, preferred_element_type=F32)
    return out


def _filter_kernel(feat_ref, w1_ref, b1_ref, f1_ref, w2_ref, b2_ref, f2_ref, w3_ref, dl_ref, o_ref, ss_ref):
    feats = feat_ref[...]
    feats = feats.reshape(-1, feats.shape[-1])
    h = jnp.sin(f1_ref[...] * (_dot3(feats, w1_ref[...]) + b1_ref[...]))
    h = jnp.sin(f2_ref[...] * (_dot3(h, w2_ref[...]) + b2_ref[...]))
    filt = jnp.dot(h.astype(BF16), w3_ref[...].astype(BF16), preferred_element_type=F32)
    tpos = feats[:, 0:1]
    live = feats[:, LANES - 1:LANES]
    val = filt * jnp.exp(-tpos * jnp.abs(dl_ref[...])) * live

    @pl.when((pl.program_id(0) == 0) & (pl.program_id(1) == 0))
    def _():
        ss_ref[...] = jnp.zeros_like(ss_ref)

    ss_ref[...] += jnp.sum(val * val, axis=0, keepdims=True)
    o_ref[...] = val.reshape(o_ref.shape).astype(BF16)


def _filter_time(feats, w1p, b1, f1, w2, b2, f2, w3, deltas, t2_blk=8):
    _, n1, _ = feats.shape
    c = w3.shape[1] // 2
    hid = w2.shape[0]
    const = lambda i, h: (0, 0)
    return pl.pallas_call(
        _filter_kernel,
        grid=(DFT_MINOR // t2_blk, 2),
        in_specs=[
            pl.BlockSpec((t2_blk, n1 // 2, LANES), lambda i, h: (i, h, 0)),
            pl.BlockSpec((LANES, hid), const),
            pl.BlockSpec((1, hid), const),
            pl.BlockSpec((1, hid), const),
            pl.BlockSpec((hid, hid), const),
            pl.BlockSpec((1, hid), const),
            pl.BlockSpec((1, hid), const),
            pl.BlockSpec((hid, c), lambda i, h: (0, h)),
            pl.BlockSpec((1, c), const),
        ],
        out_specs=[pl.BlockSpec((t2_blk, n1 // 2, c), lambda i, h: (i, h, 0)), pl.BlockSpec((1, c), const)],
        out_shape=[jax.ShapeDtypeStruct((DFT_MINOR, n1, c), BF16), jax.ShapeDtypeStruct((1, c), F32)],
        compiler_params=_params(("arbitrary", "arbitrary")),
        name="filter_time",
    )(feats, w1p, b1, f1, w2, b2, f2, w3, deltas)


def _dft_a_kernel(x_ref, tab_ref, o_ref):
    x = x_ref[...]
    x = x.reshape(-1, x.shape[-1])
    o_ref[0] = jnp.dot(tab_ref[0], x, preferred_element_type=F32).astype(o_ref.dtype)


def _dft_stage_a(x3, tab, n_out_rows, t2_major=False):
    if t2_major:
        _, r, c = x3.shape
        g = 1
        x_spec = pl.BlockSpec((1, r, c), lambda t2: (t2, 0, 0))
    else:
        g, r, wc = x3.shape
        c = wc // DFT_MINOR
        x_spec = pl.BlockSpec((g, r, c), lambda t2: (0, 0, t2))
    return pl.pallas_call(
        _dft_a_kernel,
        grid=(DFT_MINOR,),
        in_specs=[x_spec,
                  pl.BlockSpec((1, n_out_rows, g * r), lambda t2: (t2, 0, 0))],
        out_specs=pl.BlockSpec((1, n_out_rows, c), lambda t2: (t2, 0, 0)),
        out_shape=jax.ShapeDtypeStruct((DFT_MINOR, n_out_rows, c), BF16),
        compiler_params=_params(("parallel",)),
        name="dft_stage_a",
    )(x3, tab)


def _filt_b_kernel(re_ref, im_ref, mf_ref, ss_ref, o_ref, *, n_total):
    rhs = jnp.concatenate([re_ref[...], im_ref[...]], axis=0)
    spec = jnp.dot(mf_ref[...], rhs, preferred_element_type=F32)
    scale = lax.rsqrt(ss_ref[...] + FILTER_EPS) * (1.0 / n_total)
    o_ref[0] = (spec * scale).astype(o_ref.dtype)


def _filter_stage_b(a_arr, mf, sumsq, n1):
    _, rows, c = a_arr.shape
    a2 = a_arr.reshape(DFT_MINOR, rows * c)
    n_total = n1 * DFT_MINOR
    return pl.pallas_call(
        functools.partial(_filt_b_kernel, n_total=n_total),
        grid=(n1,),
        in_specs=[pl.BlockSpec((DFT_MINOR, c), lambda k1: (0, k1)),
                  pl.BlockSpec((DFT_MINOR, c), lambda k1: (0, n1 + k1)),
                  pl.BlockSpec((2 * DFT_MINOR, 2 * DFT_MINOR), lambda k1: (0, 0)),
                  pl.BlockSpec((1, c), lambda k1: (0, 0))],
        out_specs=pl.BlockSpec((1, 2 * DFT_MINOR, c), lambda k1: (k1, 0, 0)),
        out_shape=jax.ShapeDtypeStruct((n1, 2 * DFT_MINOR, c), BF16),
        compiler_params=_params(("parallel",)),
        name="filter_stage_b",
    )(a2, a2, mf, sumsq)


def _conv_b_kernel(re_ref, im_ref, mf_ref, mi_ref, h_ref, o_ref):
    half = DFT_MINOR
    rhs = jnp.concatenate([re_ref[...], im_ref[...]], axis=0)
    spec = jnp.dot(mf_ref[...], rhs, preferred_element_type=F32)
    hh = h_ref[0].astype(F32)
    xr, xi = spec[:half], spec[half:]
    hr, hi = hh[:half], hh[half:]
    yr = xr * hr - xi * hi
    yi = xr * hi + xi * hr
    y = jnp.concatenate([yr, yi], axis=0).astype(BF16)
    o_ref[0] = jnp.dot(mi_ref[...], y, preferred_element_type=F32).astype(o_ref.dtype)


def _conv_stage_b(a_arr, mf, mi, hspec, n1):
    _, rows, c = a_arr.shape
    a2 = a_arr.reshape(DFT_MINOR, rows * c)
    return pl.pallas_call(
        _conv_b_kernel,
        grid=(n1,),
        in_specs=[pl.BlockSpec((DFT_MINOR, c), lambda k1: (0, k1)),
                  pl.BlockSpec((DFT_MINOR, c), lambda k1: (0, n1 + k1)),
                  pl.BlockSpec((2 * DFT_MINOR, 2 * DFT_MINOR), lambda k1: (0, 0)),
                  pl.BlockSpec((2 * DFT_MINOR, 2 * DFT_MINOR), lambda k1: (0, 0)),
                  pl.BlockSpec((1, 2 * DFT_MINOR, c), lambda k1: (k1, 0, 0))],
        out_specs=pl.BlockSpec((1, 2 * DFT_MINOR, c), lambda k1: (k1, 0, 0)),
        out_shape=jax.ShapeDtypeStruct((n1, 2 * DFT_MINOR, c), BF16),
        compiler_params=_params(("parallel",)),
        name="conv_stage_b",
    )(a2, a2, mf, mi, hspec)


def _conv_out_kernel(re_ref, im_ref, tab_ref, o_ref):
    rhs = jnp.concatenate([re_ref[...], im_ref[...]], axis=0)
    y = jnp.dot(tab_ref[0], rhs, preferred_element_type=F32)
    o_ref[...] = y.reshape(o_ref.shape).astype(o_ref.dtype)


def _conv_stage_out(b_arr, tab, batch, n1):
    _, rows, c = b_arr.shape
    b2 = b_arr.reshape(n1, rows * c)
    r = n1 // 2
    return pl.pallas_call(
        _conv_out_kernel,
        grid=(DFT_MINOR,),
        in_specs=[pl.BlockSpec((n1, c), lambda t2: (0, t2)),
                  pl.BlockSpec((n1, c), lambda t2: (0, DFT_MINOR + t2)),
                  pl.BlockSpec((1, batch * r, 2 * n1), lambda t2: (t2, 0, 0))],
        out_specs=pl.BlockSpec((batch, r, c), lambda t2: (0, 0, t2)),
        out_shape=jax.ShapeDtypeStruct((batch, r, DFT_MINOR * c), BF16),
        compiler_params=_params(("parallel",)),
        name="conv_stage_out",
    )(b2, b2, tab)


def _dft_tables(n1, n_sig_rows):
    n = n1 * DFT_MINOR
    k1 = np.arange(n1, dtype=np.int64)[None, :, None]
    t2 = np.arange(DFT_MINOR, dtype=np.int64)[:, None, None]

    def twiddled(n_t1):
        t1 = np.arange(n_t1, dtype=np.int64)[None, None, :]
        m = (k1 * (DFT_MINOR * t1 + t2)) % n
        ang = m.astype(np.float64) * (-2.0 * math.pi / n)
        return np.cos(ang).astype(np.float32), np.sin(ang).astype(np.float32)

    pr, pi = twiddled(n_sig_rows)
    tab_sig = np.concatenate([np.concatenate([pr, -pi], axis=2), np.concatenate([pi, pr], axis=2)], axis=1)
    prt, pit = np.swapaxes(pr, 1, 2), np.swapaxes(pi, 1, 2)
    tab_out = np.concatenate([np.concatenate([prt, pit], axis=2), np.concatenate([-pit, prt], axis=2)], axis=1)
    fr, fi = twiddled(n1)
    tab_filt = np.concatenate([fr, fi], axis=1)
    j = np.arange(DFT_MINOR)
    ang = -2.0 * np.pi * ((j[:, None] * j[None, :]) % DFT_MINOR) / DFT_MINOR
    cr, ci = np.cos(ang), np.sin(ang)
    mf = np.block([[cr, -ci], [ci, cr]])
    mi = np.block([[cr, ci], [-ci, cr]])
    return tuple(jnp.asarray(a, BF16) for a in (tab_sig, tab_out, tab_filt, mf, mi))


def _merge_a_kernel(a_ref, yc_ref, z_ref, x0_ref, d_ref, ga_ref, gh_ref, bga_ref, bgh_ref, wa_ref, wh_ref, o_ref, hy_scr):
    @pl.when(pl.program_id(1) == 0)
    def _():
        z = z_ref[...].astype(F32)
        hy = x0_ref[...].astype(F32) * (yc_ref[...].astype(F32) + z * d_ref[...])
        hy_scr[...] = hy.astype(BF16)

    ya = jnp.dot(a_ref[...], wa_ref[...], preferred_element_type=F32)
    yh = jnp.dot(hy_scr[...], wh_ref[...], preferred_element_type=F32)
    ga = jax.nn.sigmoid(ga_ref[...].astype(F32) + bga_ref[...])
    gh = jax.nn.sigmoid(gh_ref[...].astype(F32) + bgh_ref[...])
    o_ref[...] = (ga * ya + gh * yh).astype(BF16)


def _merge_a(attn, yconv, z, x0c, d_bias, proj, b_gate2, wa, wh, gate_col0, tm=512, tn=512):
    t, aw = attn.shape
    hw = yconv.shape[1]
    d = wa.shape[1]
    assert gate_col0 % tn == 0 and d % tn == 0
    gb = gate_col0 // tn
    nj = d // tn
    hy_spec = pl.BlockSpec((tm, hw), lambda i, j: (i, 0))
    return pl.pallas_call(
        _merge_a_kernel,
        grid=(t // tm, nj),
        in_specs=[pl.BlockSpec((tm, aw), lambda i, j: (i, 0)),
                  hy_spec, hy_spec, hy_spec,
                  pl.BlockSpec((1, hw), lambda i, j: (0, 0)),
                  pl.BlockSpec((tm, tn), lambda i, j: (i, gb + j)),
                  pl.BlockSpec((tm, tn), lambda i, j: (i, gb + nj + j)),
                  pl.BlockSpec((1, tn), lambda i, j: (0, j)),
                  pl.BlockSpec((1, tn), lambda i, j: (0, nj + j)),
                  pl.BlockSpec((aw, tn), lambda i, j: (0, j)),
                  pl.BlockSpec((hw, tn), lambda i, j: (0, j))],
        out_specs=pl.BlockSpec((tm, tn), lambda i, j: (i, j)),
        out_shape=jax.ShapeDtypeStruct((t, d), BF16),
        scratch_shapes=[pltpu.VMEM((tm, hw), BF16)],
        compiler_params=_params(("parallel", "arbitrary")),
        name="merge_gated",
    )(attn, yconv, z, x0c, d_bias, proj, proj, b_gate2, b_gate2, wa, wh)


def _merge_b_kernel(u_ref, x_ref, gi_ref, bi_ref, wo_ref, g1_ref, b1_ref, wr_ref, br_ref, h_ref, lg_ref):
    mixed = jnp.dot(u_ref[...], wo_ref[...], preferred_element_type=F32)
    h0 = _layer_norm(x_ref[...], gi_ref[...], bi_ref[...])
    h1 = _layer_norm(DEEPNORM_ALPHA * h0 + mixed, g1_ref[...], b1_ref[...])
    h_ref[...] = h1
    wr = wr_ref[...]
    h_hi = h1.astype(BF16)
    h_lo = (h1 - h_hi.astype(F32)).astype(BF16)
    w_hi = wr.astype(BF16)
    w_lo = (wr - w_hi.astype(F32)).astype(BF16)
    lg = jnp.dot(h_hi, w_hi, preferred_element_type=F32)
    lg += jnp.dot(h_hi, w_lo, preferred_element_type=F32)
    lg += jnp.dot(h_lo, w_hi, preferred_element_type=F32)
    lg_ref[...] = lg + br_ref[...]


def _merge_b(u, x2, gi, bi, wo, g1, b1, wr, br, tm=512):
    t, d = x2.shape
    const = lambda i: (0, 0)
    return pl.pallas_call(
        _merge_b_kernel,
        grid=(t // tm,),
        in_specs=[pl.BlockSpec((tm, d), lambda i: (i, 0)),
                  pl.BlockSpec((tm, d), lambda i: (i, 0)),
                  pl.BlockSpec((1, d), const), pl.BlockSpec((1, d), const),
                  pl.BlockSpec((d, d), const),
                  pl.BlockSpec((1, d), const), pl.BlockSpec((1, d), const),
                  pl.BlockSpec((d, LANES), const), pl.BlockSpec((1, LANES), const)],
        out_specs=[pl.BlockSpec((tm, d), lambda i: (i, 0)), pl.BlockSpec((tm, LANES), lambda i: (i, 0))],
        out_shape=[jax.ShapeDtypeStruct((t, d), F32), jax.ShapeDtypeStruct((t, LANES), F32)],
        compiler_params=_params(("parallel",)),
        name="merge_out_ln1",
    )(u, x2, gi, bi, wo, g1, b1, wr, br)


def _route_kernel(lg_ref, tri_ref, id_ref, gt_ref, cnt_ref, pre_scr):
    x = lg_ref[...]
    lane = lax.broadcasted_iota(jnp.int32, x.shape, 1).astype(F32)
    big = jnp.float32(1 << 20)
    neg = -jnp.inf
    cmask = lane < N_GROUPS
    cm = jnp.max(jnp.where(cmask, x, neg), axis=-1, keepdims=True)
    grp = jnp.min(jnp.where(cmask & (x == cm), lane, big), axis=-1, keepdims=True)
    csum = jnp.sum(jnp.where(cmask, jnp.exp(x - cm), 0.0), axis=-1, keepdims=True)
    p_grp = 1.0 / csum
    lo = N_GROUPS + grp * EXPERTS_PER_GROUP
    fmask = (lane >= lo) & (lane < lo + EXPERTS_PER_GROUP)
    f1 = jnp.max(jnp.where(fmask, x, neg), axis=-1, keepdims=True)
    i1 = jnp.min(jnp.where(fmask & (x == f1), lane, big), axis=-1, keepdims=True)
    mask2 = fmask & (lane != i1)
    f2 = jnp.max(jnp.where(mask2, x, neg), axis=-1, keepdims=True)
    i2 = jnp.min(jnp.where(mask2 & (x == f2), lane, big), axis=-1, keepdims=True)
    e2 = jnp.exp(f2 - f1)
    g1 = p_grp / (1.0 + e2)
    g2 = p_grp * e2 / (1.0 + e2)
    gt_ref[...] = jnp.where(lane == 0, g1, jnp.where(lane == 1, g2, 0.0))

    @pl.when(pl.program_id(0) == 0)
    def _():
        cnt_ref[...] = jnp.zeros_like(cnt_ref)

    sel1 = lane == i1
    sel2 = lane == i2
    chosen = jnp.where(sel1 | sel2, 1.0, 0.0)
    sub = tri_ref.shape[0]
    carry = cnt_ref[...]
    for s in range(x.shape[0] // sub):
        cs = chosen[s * sub:(s + 1) * sub]
        pre_scr[s * sub:(s + 1) * sub, :] = jnp.dot(tri_ref[...], cs.astype(BF16), preferred_element_type=F32) + carry
        carry = carry + jnp.sum(cs, axis=0, keepdims=True)
    cnt_ref[...] = carry
    before = pre_scr[...]
    r1 = jnp.sum(jnp.where(sel1, before, 0.0), axis=-1, keepdims=True)
    r2 = jnp.sum(jnp.where(sel2, before, 0.0), axis=-1, keepdims=True)
    ids = jnp.where(lane == 0, i1 - N_GROUPS, jnp.where(lane == 1, i2 - N_GROUPS,
                    jnp.where(lane == 2, r1, jnp.where(lane == 3, r2, 0.0))))
    id_ref[...] = ids.astype(jnp.int32)


def _route(logits, tm=2048, sub=256):
    t = logits.shape[0]
    spec = pl.BlockSpec((tm, LANES), lambda i: (i, 0))
    tri = jnp.asarray(np.tril(np.ones((sub, sub), np.float32), -1), BF16)
    return pl.pallas_call(
        _route_kernel,
        grid=(t // tm,),
        in_specs=[spec, pl.BlockSpec((sub, sub), lambda i: (0, 0))],
        out_specs=[spec, spec, pl.BlockSpec((1, LANES), lambda i: (0, 0))],
        out_shape=[jax.ShapeDtypeStruct((t, LANES), jnp.int32), jax.ShapeDtypeStruct((t, LANES), F32),
                   jax.ShapeDtypeStruct((1, LANES), F32)],
        scratch_shapes=[pltpu.VMEM((tm, LANES), F32)],
        compiler_params=_params(("arbitrary",)),
        name="route_topk",
    )(logits, tri)


def _invert_kernel(d0_ref, d1_ref, lo_ref, hi_ref, o_ref):
    n_tok = d0_ref.shape[0]

    def zero(i, c):
        o_ref[i] = 0
        return c

    for e in range(lo_ref.shape[0]):
        lax.fori_loop(lo_ref[e], hi_ref[e], zero, 0)

    def put(t, c):
        o_ref[d0_ref[t]] = t
        o_ref[d1_ref[t]] = t
        return c
    lax.fori_loop(0, n_tok, put, 0, unroll=8)


def _invert(dest0, dest1, pad_lo, pad_hi, n_rows):
    return pl.pallas_call(
        _invert_kernel,
        grid_spec=pltpu.PrefetchScalarGridSpec(
            num_scalar_prefetch=4, grid=(1,), in_specs=[],
            out_specs=pl.BlockSpec(memory_space=pltpu.SMEM)),
        out_shape=jax.ShapeDtypeStruct((n_rows,), jnp.int32),
        compiler_params=_params(("arbitrary",)),
        name="dispatch_invert",
    )(dest0, dest1, pad_lo, pad_hi)


def _row_gather(src_hbm, idx_ref, base, dst, sem, n_rows):
    for r in range(n_rows):
        tok = idx_ref[base + r]
        pltpu.make_async_copy(src_hbm.at[pl.ds(tok, 1)], dst.at[pl.ds(r, 1)], sem).start()


def _row_gather_wait(src_hbm, dst, sem, n_rows):
    for r in range(n_rows):
        pltpu.make_async_copy(src_hbm.at[pl.ds(0, 1)], dst.at[pl.ds(r, 1)], sem).wait()


def _moe_kernel(be_ref, nb_ref, tok_ref, h_hbm, wg_ref, wu_ref, wd_ref, o_ref, xbuf, sem):
    i = pl.program_id(0)
    n_used = nb_ref[0]
    slot = i % 2
    rows = xbuf.shape[1]

    @pl.when(i == 0)
    def _():
        _row_gather(h_hbm, tok_ref, 0, xbuf.at[0], sem.at[0], rows)

    @pl.when(i < n_used)
    def _():
        _row_gather(h_hbm, tok_ref, (i + 1) * rows, xbuf.at[1 - slot], sem.at[1 - slot], rows)
        _row_gather_wait(h_hbm, xbuf.at[slot], sem.at[slot], rows)
        x = xbuf[slot].astype(BF16)
        g = jnp.dot(x, wg_ref[0], preferred_element_type=F32)
        u = jnp.dot(x, wu_ref[0], preferred_element_type=F32)
        hdn = (g * jax.nn.sigmoid(g) * u).astype(BF16)
        o_ref[...] = jnp.dot(hdn, wd_ref[0], preferred_element_type=F32)

    @pl.when(i == n_used)
    def _():
        _row_gather_wait(h_hbm, xbuf.at[slot], sem.at[slot], rows)

    @pl.when(i >= n_used)
    def _():
        o_ref[...] = jnp.zeros_like(o_ref)


def _moe_ffn(blk_expert, n_used, row_tok, h1, wg, wu, wd, rows=MOE_ROWS):
    t, d = h1.shape
    f = wg.shape[2]
    n_blk = blk_expert.shape[0]
    assert row_tok.shape[0] == (n_blk + 1) * rows
    grid_spec = pltpu.PrefetchScalarGridSpec(
        num_scalar_prefetch=3,
        grid=(n_blk,),
        in_specs=[pl.BlockSpec(memory_space=pl.ANY),
                  pl.BlockSpec((1, d, f), lambda i, be, nb, tk: (be[i], 0, 0)),
                  pl.BlockSpec((1, d, f), lambda i, be, nb, tk: (be[i], 0, 0)),
                  pl.BlockSpec((1, f, d), lambda i, be, nb, tk: (be[i], 0, 0))],
        out_specs=pl.BlockSpec((rows, d), lambda i, be, nb, tk: (i, 0)),
        scratch_shapes=[pltpu.VMEM((2, rows, d), F32), pltpu.SemaphoreType.DMA((2,))],
    )
    return pl.pallas_call(
        _moe_kernel,
        grid_spec=grid_spec,
        out_shape=jax.ShapeDtypeStruct((n_blk * rows, d), F32),
        compiler_params=_params(("arbitrary",)),
        name="moe_ffn",
    )(blk_expert, n_used, row_tok, h1, wg, wu, wd)


def _combine_kernel(p0_ref, p1_ref, ys_hbm, h_ref, gt_ref, g2_ref, b2_ref, o_ref, ybuf, sem):
    i = pl.program_id(0)
    n = pl.num_programs(0)
    slot = i % 2
    tm = h_ref.shape[0]

    def issue(step, s):
        _row_gather(ys_hbm, p0_ref, step * tm, ybuf.at[s, 0], sem.at[s], tm)
        _row_gather(ys_hbm, p1_ref, step * tm, ybuf.at[s, 1], sem.at[s], tm)

    def drain(s):
        _row_gather_wait(ys_hbm, ybuf.at[s, 0], sem.at[s], tm)
        _row_gather_wait(ys_hbm, ybuf.at[s, 1], sem.at[s], tm)

    @pl.when(i == 0)
    def _():
        issue(0, 0)

    issue(i + 1, 1 - slot)
    drain(slot)
    gt = gt_ref[...]
    moe = gt[:, 0:1] * ybuf[slot, 0] + gt[:, 1:2] * ybuf[slot, 1]
    o_ref[...] = _layer_norm(DEEPNORM_ALPHA * h_ref[...] + moe, g2_ref[...], b2_ref[...])

    @pl.when(i == n - 1)
    def _():
        drain(1 - slot)


def _combine(pos0, pos1, ys, h1, gates, g2, b2, tm=256):
    t, d = h1.shape
    assert pos0.shape[0] == t + tm and pos1.shape[0] == t + tm
    const = lambda i, a, b: (0, 0)
    grid_spec = pltpu.PrefetchScalarGridSpec(
        num_scalar_prefetch=2,
        grid=(t // tm,),
        in_specs=[pl.BlockSpec(memory_space=pl.ANY),
                  pl.BlockSpec((tm, d), lambda i, a, b: (i, 0)),
                  pl.BlockSpec((tm, LANES), lambda i, a, b: (i, 0)),
                  pl.BlockSpec((1, d), const), pl.BlockSpec((1, d), const)],
        out_specs=pl.BlockSpec((tm, d), lambda i, a, b: (i, 0)),
        scratch_shapes=[pltpu.VMEM((2, 2, tm, d), F32), pltpu.SemaphoreType.DMA((2,))],
    )
    return pl.pallas_call(
        _combine_kernel,
        grid_spec=grid_spec,
        out_shape=jax.ShapeDtypeStruct((t, d), F32),
        compiler_params=_params(("arbitrary",)),
        name="moe_combine_ln2",
    )(pos0, pos1, ys, h1, gates, g2, b2)


def _rope_tables(seq):
    rows = seq // GRID_W
    row_idx = jnp.repeat(jnp.arange(rows, dtype=jnp.int32), GRID_W).astype(F32)
    col_idx = jnp.tile(jnp.arange(GRID_W, dtype=jnp.int32), rows).astype(F32)
    half = HEAD_DIM // 2
    inv_freq = ROPE_THETA ** (-jnp.arange(0, half, 2, dtype=F32) / half)
    ang_r = row_idx[:, None] * inv_freq[None, :]
    ang_c = col_idx[:, None] * inv_freq[None, :]
    cr, sr, cc, sc = jnp.cos(ang_r), jnp.sin(ang_r), jnp.cos(ang_c), jnp.sin(ang_c)
    cs = jnp.concatenate([cr, cc, cr, cc], axis=-1)
    sn = jnp.concatenate([-sr, -sc, sr, sc], axis=-1)
    return cs, sn


def _pair_split(v):
    lead = v.shape[:-1]
    q4 = v.reshape(lead + (-1, 2, 2, HEAD_DIM // 4))
    return jnp.swapaxes(q4, -3, -2).reshape(v.shape)


def _filter_features(seq):
    n1 = 2 * seq // DFT_MINOR
    n = (DFT_MINOR * np.arange(n1)[None, :] + np.arange(DFT_MINOR)[:, None]).astype(np.int64)
    j = np.where(n <= seq, n, 2 * seq - n)
    j = np.where(n == seq, 0, j).astype(np.float64)
    t = (j / (seq - 1))[..., None]
    w = (2.0 * math.pi * j / seq)[..., None]
    bands = np.linspace(1e-4, FILTER_BANDS - 1, FILTER_BANDS, dtype=np.float32).astype(np.float64)
    feats = np.zeros(n.shape + (LANES,), np.float32)
    feats[..., 0:1] = t
    feats[..., 1:1 + FILTER_BANDS] = np.cos(bands * w)
    feats[..., 1 + FILTER_BANDS:FILTER_EMB] = -np.sin(bands * w)
    feats[..., LANES - 1] = (n != seq)
    return jnp.asarray(feats)


def _dispatch_plan(ids, counts, n_tok, rows, tail):
    counts = counts.astype(jnp.int32)
    padded = (counts + rows - 1) // rows * rows
    pad_end = jnp.cumsum(padded)
    pad_start = pad_end - padded
    n_blk = (n_tok * TOP_K + N_EXPERTS * rows) // rows + 1
    blk_start = jnp.arange(n_blk, dtype=jnp.int32) * rows
    blk_expert = jnp.minimum(jnp.sum((pad_end[None, :] <= blk_start[:, None]).astype(jnp.int32), axis=1),
                             N_EXPERTS - 1)
    n_used = (pad_end[-1] // rows).astype(jnp.int32).reshape(1)
    experts = jnp.arange(N_EXPERTS, dtype=jnp.int32)[None, :]

    def dest(e, rank):
        start = jnp.sum(jnp.where(e[:, None] == experts, pad_start[None, :], 0), axis=1)
        return jnp.pad(start + rank, (0, tail))

    pos0 = dest(ids[:, 0], ids[:, 2])
    pos1 = dest(ids[:, 1], ids[:, 3])
    n_map = (n_blk + 1) * rows
    pad_lo = pad_start + counts
    pad_hi = pad_end.at[N_EXPERTS - 1].set(n_map)
    row_tok = _invert(pos0[:n_tok], pos1[:n_tok], pad_lo, pad_hi, n_map)
    return blk_expert, n_used, row_tok, pos0, pos1


def kernel(x, ln_in_g, ln_in_b, w_in, b_gate, q_norm_g, k_norm_g, hy_conv_w, hy_conv_b, filt_w1, filt_b1, filt_f1, filt_w2, filt_b2, filt_f2, filt_w3, hy_bias_d, w_attn_o, w_hy_o, w_out, ln1_g, ln1_b, w_route_grp, b_route_grp, w_route_exp, b_route_exp, w_exp_gate, w_exp_up, w_exp_down, ln2_g, ln2_b):
    batch, seq, d = x.shape
    assert batch == 2, "the long convolution packs exactly two batch rows as one complex signal"
    t = batch * seq
    hw = hy_bias_d.shape[1]
    l = 0
    x2 = x.reshape(t, d)
    row = lambda v: v.reshape(1, -1)

    cs, sn = _rope_tables(seq)
    qg = row(q_norm_g[l]) * (HEAD_DIM ** -0.5 * math.log2(math.e))
    qg = _pair_split(qg)
    kg = _pair_split(row(k_norm_g[l]))
    n_qk = ATTN_WIDTH + KV_WIDTH
    w_bf = jnp.concatenate([_pair_split(w_in[l][:, :n_qk]).astype(BF16), w_in[l][:, n_qk:].astype(BF16)], axis=1)
    proj = _ln_inproj(x2, row(ln_in_g), row(ln_in_b), w_bf, qg, kg, cs, sn, seq)

    attn = _attention(proj, batch, seq)

    hy_col0 = ATTN_WIDTH + 2 * KV_WIDTH
    z, x0c = _hy_pre(proj, hy_conv_w[l], row(hy_conv_b[l]), batch, seq, hw, hy_col0)
    n1 = 2 * seq // DFT_MINOR
    tab_sig, tab_out, tab_filt, mf, mi = _dft_tables(n1, n1 // 2)
    feats = _filter_features(seq)
    w1p = jnp.pad(filt_w1[l], ((0, LANES - FILTER_EMB), (0, 0)))
    min_decay = math.log(DECAY_TARGET) / SLOW_DECAY_PCT
    max_decay = math.log(DECAY_TARGET) / FAST_DECAY_PCT
    deltas = jnp.linspace(min_decay, max_decay, hw, dtype=F32)[None, :]
    two_sided, sumsq = _filter_time(feats, w1p, row(filt_b1[l]), row(filt_f1[l]), filt_w2[l], row(filt_b2[l]),
                                    row(filt_f2[l]), filt_w3[l], deltas)
    fa = _dft_stage_a(two_sided, tab_filt, 2 * n1, t2_major=True)
    hspec = _filter_stage_b(fa, mf, sumsq, n1)
    za = _dft_stage_a(z.reshape(batch, n1 // 2, DFT_MINOR * hw), tab_sig, 2 * n1)
    zb = _conv_stage_b(za, mf, mi, hspec, n1)
    yconv = _conv_stage_out(zb, tab_out, batch, n1).reshape(t, hw)

    gate_col0 = hy_col0 + 3 * hw
    u = _merge_a(attn, yconv, z, x0c, row(hy_bias_d[l]), proj, row(b_gate[l]), w_attn_o[l].astype(BF16),
                 w_hy_o[l].astype(BF16), gate_col0)
    n_r = N_GROUPS + N_EXPERTS
    wr = jnp.pad(jnp.concatenate([w_route_grp[l], w_route_exp[l]], axis=1), ((0, 0), (0, LANES - n_r)))
    br = jnp.pad(jnp.concatenate([b_route_grp[l], b_route_exp[l]]), (0, LANES - n_r)).reshape(1, LANES)
    h1, logits = _merge_b(u, x2, row(ln_in_g), row(ln_in_b), w_out[l].astype(BF16), row(ln1_g[l]), row(ln1_b[l]),
                          wr, br)

    ids, gates, cnt = _route(logits)
    counts = cnt[0, N_GROUPS:N_GROUPS + N_EXPERTS]
    combine_tile = 256
    blk_expert, n_used, row_tok, pos0, pos1 = _dispatch_plan(ids, counts, t, MOE_ROWS, combine_tile)
    ys = _moe_ffn(blk_expert, n_used, row_tok, h1, w_exp_gate[l].astype(BF16), w_exp_up[l].astype(BF16),
                  w_exp_down[l].astype(BF16))
    out = _combine(pos0, pos1, ys, h1, gates, row(ln2_g[l]), row(ln2_b[l]), tm=combine_tile)
    return out.reshape(batch, seq, d)
```

```python
import functools
import math

import jax
import jax.numpy as jnp
import numpy as np
from jax import lax
from jax.experimental import pallas as pl
from jax.experimental.pallas import tpu as pltpu

F32 = jnp.float32
BF16 = jnp.bfloat16

GRID_W = 64
N_Q_HEADS = 8
N_KV_HEADS = 2
HEAD_DIM = 128
Q_GROUP = N_Q_HEADS // N_KV_HEADS
ATTN_WIDTH = N_Q_HEADS * HEAD_DIM
KV_WIDTH = N_KV_HEADS * HEAD_DIM
ROPE_THETA = 10000.0
QK_EPS = 1e-6
SHORT_CONV = 3
FILTER_BANDS = 16
FILTER_EMB = 1 + 2 * FILTER_BANDS
DECAY_TARGET = 1e-2
FAST_DECAY_PCT = 0.3
SLOW_DECAY_PCT = 1.5
FILTER_EPS = 1e-6
N_GROUPS = 4
EXPERTS_PER_GROUP = 8
N_EXPERTS = N_GROUPS * EXPERTS_PER_GROUP
TOP_K = 2
LN_EPS = 1e-5
DEPTH = 1
DEEPNORM_ALPHA = (2 * DEPTH) ** 0.25

LANES = 128
V7X_VMEM_LIMIT = 56 * 1024 * 1024
DFT_MINOR = 128

MOE_ROWS = 256


def _params(sem, vmem=V7X_VMEM_LIMIT):
    return pltpu.CompilerParams(dimension_semantics=sem, vmem_limit_bytes=vmem)


def _layer_norm(x, g, b):
    mu = jnp.mean(x, axis=-1, keepdims=True)
    xc = x - mu
    var = jnp.mean(xc * xc, axis=-1, keepdims=True)
    return xc * lax.rsqrt(var + LN_EPS) * g + b


def _norm_rope(acc, gain, cs, sn):
    ms = jnp.mean(acc * acc, axis=-1, keepdims=True)
    xn = acc * lax.rsqrt(ms + QK_EPS) * gain
    return xn * cs + pltpu.roll(xn, HEAD_DIM // 2, 1) * sn


def _ln_inproj_kernel(x_ref, g_ref, b_ref, w_ref, qg_ref, kg_ref, cs_ref, sn_ref, o_ref, h_scr, *, tn):
    j = pl.program_id(1)
    nchunk = tn // HEAD_DIM
    n_q = ATTN_WIDTH // HEAD_DIM
    n_qk = n_q + N_KV_HEADS

    @pl.when(j == 0)
    def _():
        h = _layer_norm(x_ref[...], g_ref[...], b_ref[...])
        h_scr[...] = h.astype(BF16)

    acc = jnp.dot(h_scr[...], w_ref[...], preferred_element_type=F32)

    def store(first_plain_chunk_fn):
        for c in range(nchunk):
            sl = slice(c * HEAD_DIM, (c + 1) * HEAD_DIM)
            kind = first_plain_chunk_fn(c)
            if kind == "q":
                o_ref[:, sl] = _norm_rope(acc[:, sl], qg_ref[...], cs_ref[...], sn_ref[...]).astype(BF16)
            elif kind == "k":
                o_ref[:, sl] = _norm_rope(acc[:, sl], kg_ref[...], cs_ref[...], sn_ref[...]).astype(BF16)
            else:
                o_ref[:, sl] = acc[:, sl].astype(BF16)

    n_q_tiles = n_q // nchunk
    assert n_q % nchunk == 0 and N_KV_HEADS <= nchunk

    @pl.when(j < n_q_tiles)
    def _():
        store(lambda c: "q")

    @pl.when(j == n_q_tiles)
    def _():
        store(lambda c: "k" if c < N_KV_HEADS else "p")

    @pl.when(j > n_q_tiles)
    def _():
        store(lambda c: "p")


def _ln_inproj(x2, g, b, w_bf, qg, kg, cs, sn, seq, tm=1024, tn=512):
    t, d = x2.shape
    n = w_bf.shape[1]
    assert t % tm == 0 and n % tn == 0 and seq % tm == 0
    pos_blocks = seq // tm
    return pl.pallas_call(
        functools.partial(_ln_inproj_kernel, tn=tn),
        grid=(t // tm, n // tn),
        in_specs=[
            pl.BlockSpec((tm, d), lambda i, j: (i, 0)),
            pl.BlockSpec((1, d), lambda i, j: (0, 0)),
            pl.BlockSpec((1, d), lambda i, j: (0, 0)),
            pl.BlockSpec((d, tn), lambda i, j: (0, j)),
            pl.BlockSpec((1, HEAD_DIM), lambda i, j: (0, 0)),
            pl.BlockSpec((1, HEAD_DIM), lambda i, j: (0, 0)),
            pl.BlockSpec((tm, HEAD_DIM), lambda i, j: (i % pos_blocks, 0)),
            pl.BlockSpec((tm, HEAD_DIM), lambda i, j: (i % pos_blocks, 0)),
        ],
        out_specs=pl.BlockSpec((tm, tn), lambda i, j: (i, j)),
        out_shape=jax.ShapeDtypeStruct((t, n), BF16),
        scratch_shapes=[pltpu.VMEM((tm, d), BF16)],
        compiler_params=_params(("parallel", "arbitrary")),
        name="ln_inproj",
    )(x2, g, b, w_bf, qg, kg, cs, sn)


def _attn_kernel(q_ref, k_ref, v_ref, o_ref, vt_scr, *, tq, tk):
    seq = k_ref.shape[0]
    n_chunks = seq // tk
    m_cols = Q_GROUP * tq

    @pl.when(pl.program_id(2) == 0)
    def _():
        for c in range(n_chunks):
            vt_scr[c] = v_ref[c * tk:(c + 1) * tk, :].T

    qt = jnp.concatenate([q_ref[:, h * HEAD_DIM:(h + 1) * HEAD_DIM].T for h in range(Q_GROUP)], axis=1)

    def scores(c):
        start = pl.multiple_of(c * tk, tk)
        return jnp.dot(k_ref[pl.ds(start, tk), :], qt, preferred_element_type=F32)

    def update(c, s, m, l, acc):
        m_new = jnp.maximum(m, jnp.max(s, axis=0, keepdims=True))
        p = jnp.exp2(s - m_new)
        alpha = jnp.exp2(m - m_new)
        l_new = alpha * l + jnp.sum(p, axis=0, keepdims=True)
        acc_new = alpha * acc + jnp.dot(vt_scr[c], p.astype(BF16), preferred_element_type=F32)
        return m_new, l_new, acc_new

    def body(c, carry):
        m, l, acc, s = carry
        s_next = scores(jnp.minimum(c + 1, n_chunks - 1))
        m, l, acc = update(c, s, m, l, acc)
        return m, l, acc, s_next

    m0 = jnp.full((1, m_cols), -jnp.inf, F32)
    l0 = jnp.zeros((1, m_cols), F32)
    a0 = jnp.zeros((HEAD_DIM, m_cols), F32)
    _, l, acc, _ = lax.fori_loop(0, n_chunks, body, (m0, l0, a0, scores(0)), unroll=8)
    out = acc / l
    for h in range(Q_GROUP):
        o_ref[:, h * HEAD_DIM:(h + 1) * HEAD_DIM] = out[:, h * tq:(h + 1) * tq].T.astype(BF16)


def _attention(proj, batch, seq, tq=128, tk=512):
    t = proj.shape[0]
    nq = seq // tq
    gw = Q_GROUP * HEAD_DIM
    k_col0 = ATTN_WIDTH // HEAD_DIM
    v_col0 = (ATTN_WIDTH + KV_WIDTH) // HEAD_DIM
    return pl.pallas_call(
        functools.partial(_attn_kernel, tq=tq, tk=tk),
        grid=(batch, N_KV_HEADS, nq),
        in_specs=[
            pl.BlockSpec((tq, gw), lambda b, g, i: (b * nq + i, g)),
            pl.BlockSpec((seq, HEAD_DIM), lambda b, g, i: (b, k_col0 + g)),
            pl.BlockSpec((seq, HEAD_DIM), lambda b, g, i: (b, v_col0 + g)),
        ],
        out_specs=pl.BlockSpec((tq, gw), lambda b, g, i: (b * nq + i, g)),
        out_shape=jax.ShapeDtypeStruct((t, ATTN_WIDTH), BF16),
        scratch_shapes=[pltpu.VMEM((seq // tk, HEAD_DIM, tk), BF16)],
        compiler_params=_params(("parallel", "parallel", "arbitrary")),
        name="attention",
    )(proj, proj, proj)


def _hy_pre_kernel(x0_ref, x1_ref, hv_ref,
                   p0_ref, p1_ref, pv_ref, n0_ref, n1_ref, nv_ref,
                   w0_ref, w1_ref, wv_ref, b0_ref, b1_ref, bv_ref,
                   z_ref, x0c_ref):
    r = pl.program_id(2)
    last = pl.num_programs(2) - 1
    ts = x0_ref.shape[0]
    row = lax.broadcasted_iota(jnp.int32, x0_ref.shape, 0)

    def conv(x_ref, p_ref, n_ref, w_ref, b_ref):
        x = x_ref[...].astype(F32)
        prev_row = jnp.where(r == 0, 0.0, p_ref[7:8, :].astype(F32))
        next_row = jnp.where(r == last, 0.0, n_ref[0:1, :].astype(F32))
        up = jnp.where(row == 0, prev_row, pltpu.roll(x, 1, 0))
        dn = jnp.where(row == ts - 1, next_row, pltpu.roll(x, ts - 1, 0))
        w = w_ref[...]
        return b_ref[...] + up * w[0:1, :] + x * w[1:2, :] + dn * w[2:3, :]

    x0c = conv(x0_ref, p0_ref, n0_ref, w0_ref, b0_ref)
    x1c = conv(x1_ref, p1_ref, n1_ref, w1_ref, b1_ref)
    hvc = conv(hv_ref, pv_ref, nv_ref, wv_ref, bv_ref)
    z_ref[...] = (hvc * x1c).astype(BF16)
    x0c_ref[...] = x0c.astype(BF16)


def _hy_pre(proj, conv_w, conv_b, batch, seq, hw, col0, ts=512, tc=256):
    t = proj.shape[0]
    nr = seq // ts
    nct = hw // tc
    cb0 = col0 // tc
    halo = 8
    hb = ts // halo

    def main(off):
        return pl.BlockSpec((ts, tc), lambda b, c, r: (b * nr + r, cb0 + off * nct + c))

    def prev(off):
        return pl.BlockSpec((halo, tc), lambda b, c, r: (jnp.maximum((b * nr + r) * hb - 1, 0), cb0 + off * nct + c))

    def nxt(off):
        return pl.BlockSpec((halo, tc), lambda b, c, r: (jnp.minimum((b * nr + r + 1) * hb, t // halo - 1),
                                                       cb0 + off * nct + c))

    def wspec(off):
        return pl.BlockSpec((SHORT_CONV, tc), lambda b, c, r: (0, off * nct + c))

    def bspec(off):
        return pl.BlockSpec((1, tc), lambda b, c, r: (0, off * nct + c))

    out_spec = pl.BlockSpec((ts, tc), lambda b, c, r: (b * nr + r, c))
    return pl.pallas_call(
        _hy_pre_kernel,
        grid=(batch, nct, nr),
        in_specs=[main(0), main(1), main(2), prev(0), prev(1), prev(2), nxt(0), nxt(1), nxt(2),
                  wspec(0), wspec(1), wspec(2), bspec(0), bspec(1), bspec(2)],
        out_specs=[out_spec, out_spec],
        out_shape=[jax.ShapeDtypeStruct((t, hw), BF16), jax.ShapeDtypeStruct((t, hw), BF16)],
        compiler_params=_params(("parallel", "parallel", "parallel")),
        name="hy_pre",
    )(proj, proj, proj, proj, proj, proj, proj, proj, proj,
      conv_w, conv_w, conv_w, conv_b, conv_b, conv_b)


def _dot3(a, b):
    a_hi = a.astype(BF16)
    a_lo = (a - a_hi.astype(F32)).astype(BF16)
    b_hi = b.astype(BF16)
    b_lo = (b - b_hi.astype(F32)).astype(BF16)
    out = jnp.dot(a_hi, b_hi, preferred_element_type=F32)
    out += jnp.dot(a_hi, b_lo, preferred_element_type=F32)
    out += jnp.dot(a_lo, b_hi, preferred_element_type=F32)
    return out


def _filter_kernel(feat_ref, w1_ref, b1_ref, f1_ref, w2_ref, b2_ref, f2_ref, w3_ref, dl_ref, o_ref, ss_ref):
    feats = feat_ref[...]
    feats = feats.reshape(-1, feats.shape[-1])
    h = jnp.sin(f1_ref[...] * (_dot3(feats, w1_ref[...]) + b1_ref[...]))
    h = jnp.sin(f2_ref[...] * (_dot3(h, w2_ref[...]) + b2_ref[...]))
    filt = jnp.dot(h.astype(BF16), w3_ref[...].astype(BF16), preferred_element_type=F32)
    tpos = feats[:, 0:1]
    live = feats[:, LANES - 1:LANES]
    val = filt * jnp.exp(-tpos * jnp.abs(dl_ref[...])) * live

    @pl.when((pl.program_id(0) == 0) & (pl.program_id(1) == 0))
    def _():
        ss_ref[...] = jnp.zeros_like(ss_ref)

    ss_ref[...] += jnp.sum(val * val, axis=0, keepdims=True)
    o_ref[...] = val.reshape(o_ref.shape).astype(BF16)


def _filter_time(feats, w1p, b1, f1, w2, b2, f2, w3, deltas, t2_blk=8):
    _, n1, _ = feats.shape
    c = w3.shape[1] // 2
    hid = w2.shape[0]
    const = lambda i, h: (0, 0)
    return pl.pallas_call(
        _filter_kernel,
        grid=(DFT_MINOR // t2_blk, 2),
        in_specs=[
            pl.BlockSpec((t2_blk, n1 // 2, LANES), lambda i, h: (i, h, 0)),
            pl.BlockSpec((LANES, hid), const),
            pl.BlockSpec((1, hid), const),
            pl.BlockSpec((1, hid), const),
            pl.BlockSpec((hid, hid), const),
            pl.BlockSpec((1, hid), const),
            pl.BlockSpec((1, hid), const),
            pl.BlockSpec((hid, c), lambda i, h: (0, h)),
            pl.BlockSpec((1, c), const),
        ],
        out_specs=[pl.BlockSpec((t2_blk, n1 // 2, c), lambda i, h: (i, h, 0)), pl.BlockSpec((1, c), const)],
        out_shape=[jax.ShapeDtypeStruct((DFT_MINOR, n1, c), BF16), jax.ShapeDtypeStruct((1, c), F32)],
        compiler_params=_params(("arbitrary", "arbitrary")),
        name="filter_time",
    )(feats, w1p, b1, f1, w2, b2, f2, w3, deltas)


def _dft_a_kernel(x_ref, tab_ref, o_ref):
    x = x_ref[...]
    x = x.reshape(-1, x.shape[-1])
    o_ref[0] = jnp.dot(tab_ref[0], x, preferred_element_type=F32).astype(o_ref.dtype)


def _dft_stage_a(x3, tab, n_out_rows, t2_major=False):
    if t2_major:
        _, r, c = x3.shape
        g = 1
        x_spec = pl.BlockSpec((1, r, c), lambda t2: (t2, 0, 0))
    else:
        g, r, wc = x3.shape
        c = wc // DFT_MINOR
        x_spec = pl.BlockSpec((g, r, c), lambda t2: (0, 0, t2))
    return pl.pallas_call(
        _dft_a_kernel,
        grid=(DFT_MINOR,),
        in_specs=[x_spec,
                  pl.BlockSpec((1, n_out_rows, g * r), lambda t2: (t2, 0, 0))],
        out_specs=pl.BlockSpec((1, n_out_rows, c), lambda t2: (t2, 0, 0)),
        out_shape=jax.ShapeDtypeStruct((DFT_MINOR, n_out_rows, c), BF16),
        compiler_params=_params(("parallel",)),
        name="dft_stage_a",
    )(x3, tab)


def _filt_b_kernel(re_ref, im_ref, mf_ref, ss_ref, o_ref, *, n_total):
    rhs = jnp.concatenate([re_ref[...], im_ref[...]], axis=0)
    spec = jnp.dot(mf_ref[...], rhs, preferred_element_type=F32)
    scale = lax.rsqrt(ss_ref[...] + FILTER_EPS) * (1.0 / n_total)
    o_ref[0] = (spec * scale).astype(o_ref.dtype)


def _filter_stage_b(a_arr, mf, sumsq, n1):
    _, rows, c = a_arr.shape
    a2 = a_arr.reshape(DFT_MINOR, rows * c)
    n_total = n1 * DFT_MINOR
    return pl.pallas_call(
        functools.partial(_filt_b_kernel, n_total=n_total),
        grid=(n1,),
        in_specs=[pl.BlockSpec((DFT_MINOR, c), lambda k1: (0, k1)),
                  pl.BlockSpec((DFT_MINOR, c), lambda k1: (0, n1 + k1)),
                  pl.BlockSpec((2 * DFT_MINOR, 2 * DFT_MINOR), lambda k1: (0, 0)),
                  pl.BlockSpec((1, c), lambda k1: (0, 0))],
        out_specs=pl.BlockSpec((1, 2 * DFT_MINOR, c), lambda k1: (k1, 0, 0)),
        out_shape=jax.ShapeDtypeStruct((n1, 2 * DFT_MINOR, c), BF16),
        compiler_params=_params(("parallel",)),
        name="filter_stage_b",
    )(a2, a2, mf, sumsq)


def _conv_b_kernel(re_ref, im_ref, mf_ref, mi_ref, h_ref, o_ref):
    half = DFT_MINOR
    rhs = jnp.concatenate([re_ref[...], im_ref[...]], axis=0)
    spec = jnp.dot(mf_ref[...], rhs, preferred_element_type=F32)
    hh = h_ref[0].astype(F32)
    xr, xi = spec[:half], spec[half:]
    hr, hi = hh[:half], hh[half:]
    yr = xr * hr - xi * hi
    yi = xr * hi + xi * hr
    y = jnp.concatenate([yr, yi], axis=0).astype(BF16)
    o_ref[0] = jnp.dot(mi_ref[...], y, preferred_element_type=F32).astype(o_ref.dtype)


def _conv_stage_b(a_arr, mf, mi, hspec, n1):
    _, rows, c = a_arr.shape
    a2 = a_arr.reshape(DFT_MINOR, rows * c)
    return pl.pallas_call(
        _conv_b_kernel,
        grid=(n1,),
        in_specs=[pl.BlockSpec((DFT_MINOR, c), lambda k1: (0, k1)),
                  pl.BlockSpec((DFT_MINOR, c), lambda k1: (0, n1 + k1)),
                  pl.BlockSpec((2 * DFT_MINOR, 2 * DFT_MINOR), lambda k1: (0, 0)),
                  pl.BlockSpec((2 * DFT_MINOR, 2 * DFT_MINOR), lambda k1: (0, 0)),
                  pl.BlockSpec((1, 2 * DFT_MINOR, c), lambda k1: (k1, 0, 0))],
        out_specs=pl.BlockSpec((1, 2 * DFT_MINOR, c), lambda k1: (k1, 0, 0)),
        out_shape=jax.ShapeDtypeStruct((n1, 2 * DFT_MINOR, c), BF16),
        compiler_params=_params(("parallel",)),
        name="conv_stage_b",
    )(a2, a2, mf, mi, hspec)


def _conv_out_kernel(re_ref, im_ref, tab_ref, o_ref):
    rhs = jnp.concatenate([re_ref[...], im_ref[...]], axis=0)
    y = jnp.dot(tab_ref[0], rhs, preferred_element_type=F32)
    o_ref[...] = y.reshape(o_ref.shape).astype(o_ref.dtype)


def _conv_stage_out(b_arr, tab, batch, n1):
    _, rows, c = b_arr.shape
    b2 = b_arr.reshape(n1, rows * c)
    r = n1 // 2
    return pl.pallas_call(
        _conv_out_kernel,
        grid=(DFT_MINOR,),
        in_specs=[pl.BlockSpec((n1, c), lambda t2: (0, t2)),
                  pl.BlockSpec((n1, c), lambda t2: (0, DFT_MINOR + t2)),
                  pl.BlockSpec((1, batch * r, 2 * n1), lambda t2: (t2, 0, 0))],
        out_specs=pl.BlockSpec((batch, r, c), lambda t2: (0, 0, t2)),
        out_shape=jax.ShapeDtypeStruct((batch, r, DFT_MINOR * c), BF16),
        compiler_params=_params(("parallel",)),
        name="conv_stage_out",
    )(b2, b2, tab)


def _dft_tables(n1, n_sig_rows):
    n = n1 * DFT_MINOR
    k1 = np.arange(n1, dtype=np.int64)[None, :, None]
    t2 = np.arange(DFT_MINOR, dtype=np.int64)[:, None, None]

    def twiddled(n_t1):
        t1 = np.arange(n_t1, dtype=np.int64)[None, None, :]
        m = (k1 * (DFT_MINOR * t1 + t2)) % n
        ang = m.astype(np.float64) * (-2.0 * math.pi / n)
        return np.cos(ang).astype(np.float32), np.sin(ang).astype(np.float32)

    pr, pi = twiddled(n_sig_rows)
    tab_sig = np.concatenate([np.concatenate([pr, -pi], axis=2), np.concatenate([pi, pr], axis=2)], axis=1)
    prt, pit = np.swapaxes(pr, 1, 2), np.swapaxes(pi, 1, 2)
    tab_out = np.concatenate([np.concatenate([prt, pit], axis=2), np.concatenate([-pit, prt], axis=2)], axis=1)
    fr, fi = twiddled(n1)
    tab_filt = np.concatenate([fr, fi], axis=1)
    j = np.arange(DFT_MINOR)
    ang = -2.0 * np.pi * ((j[:, None] * j[None, :]) % DFT_MINOR) / DFT_MINOR
    cr, ci = np.cos(ang), np.sin(ang)
    mf = np.block([[cr, -ci], [ci, cr]])
    mi = np.block([[cr, ci], [-ci, cr]])
    return tuple(jnp.asarray(a, BF16) for a in (tab_sig, tab_out, tab_filt, mf, mi))


def _merge_a_kernel(a_ref, yc_ref, z_ref, x0_ref, d_ref, ga_ref, gh_ref, bga_ref, bgh_ref, wa_ref, wh_ref, o_ref, hy_scr):
    @pl.when(pl.program_id(1) == 0)
    def _():
        z = z_ref[...].astype(F32)
        hy = x0_ref[...].astype(F32) * (yc_ref[...].astype(F32) + z * d_ref[...])
        hy_scr[...] = hy.astype(BF16)

    ya = jnp.dot(a_ref[...], wa_ref[...], preferred_element_type=F32)
    yh = jnp.dot(hy_scr[...], wh_ref[...], preferred_element_type=F32)
    ga = jax.nn.sigmoid(ga_ref[...].astype(F32) + bga_ref[...])
    gh = jax.nn.sigmoid(gh_ref[...].astype(F32) + bgh_ref[...])
    o_ref[...] = (ga * ya + gh * yh).astype(BF16)


def _merge_a(attn, yconv, z, x0c, d_bias, proj, b_gate2, wa, wh, gate_col0, tm=512, tn=512):
    t, aw = attn.shape
    hw = yconv.shape[1]
    d = wa.shape[1]
    assert gate_col0 % tn == 0 and d % tn == 0
    gb = gate_col0 // tn
    nj = d // tn
    hy_spec = pl.BlockSpec((tm, hw), lambda i, j: (i, 0))
    return pl.pallas_call(
        _merge_a_kernel,
        grid=(t // tm, nj),
        in_specs=[pl.BlockSpec((tm, aw), lambda i, j: (i, 0)),
                  hy_spec, hy_spec, hy_spec,
                  pl.BlockSpec((1, hw), lambda i, j: (0, 0)),
                  pl.BlockSpec((tm, tn), lambda i, j: (i, gb + j)),
                  pl.BlockSpec((tm, tn), lambda i, j: (i, gb + nj + j)),
                  pl.BlockSpec((1, tn), lambda i, j: (0, j)),
                  pl.BlockSpec((1, tn), lambda i, j: (0, nj + j)),
                  pl.BlockSpec((aw, tn), lambda i, j: (0, j)),
                  pl.BlockSpec((hw, tn), lambda i, j: (0, j))],
        out_specs=pl.BlockSpec((tm, tn), lambda i, j: (i, j)),
        out_shape=jax.ShapeDtypeStruct((t, d), BF16),
        scratch_shapes=[pltpu.VMEM((tm, hw), BF16)],
        compiler_params=_params(("parallel", "arbitrary")),
        name="merge_gated",
    )(attn, yconv, z, x0c, d_bias, proj, proj, b_gate2, b_gate2, wa, wh)


def _merge_b_kernel(u_ref, x_ref, gi_ref, bi_ref, wo_ref, g1_ref, b1_ref, wr_ref, br_ref, h_ref, ht_ref, lg_ref):
    mixed = jnp.dot(u_ref[...], wo_ref[...], preferred_element_type=F32)
    h0 = _layer_norm(x_ref[...], gi_ref[...], bi_ref[...])
    h1 = _layer_norm(DEEPNORM_ALPHA * h0 + mixed, g1_ref[...], b1_ref[...])
    h_ref[...] = h1
    _to_token_tiles(ht_ref, h1)
    wr = wr_ref[...]
    h_hi = h1.astype(BF16)
    h_lo = (h1 - h_hi.astype(F32)).astype(BF16)
    w_hi = wr.astype(BF16)
    w_lo = (wr - w_hi.astype(F32)).astype(BF16)
    lg = jnp.dot(h_hi, w_hi, preferred_element_type=F32)
    lg += jnp.dot(h_hi, w_lo, preferred_element_type=F32)
    lg += jnp.dot(h_lo, w_hi, preferred_element_type=F32)
    lg_ref[...] = lg + br_ref[...]


def _merge_b(u, x2, gi, bi, wo, g1, b1, wr, br, tm=512):
    t, d = x2.shape
    const = lambda i: (0, 0)
    return pl.pallas_call(
        _merge_b_kernel,
        grid=(t // tm,),
        in_specs=[pl.BlockSpec((tm, d), lambda i: (i, 0)),
                  pl.BlockSpec((tm, d), lambda i: (i, 0)),
                  pl.BlockSpec((1, d), const), pl.BlockSpec((1, d), const),
                  pl.BlockSpec((d, d), const),
                  pl.BlockSpec((1, d), const), pl.BlockSpec((1, d), const),
                  pl.BlockSpec((d, LANES), const), pl.BlockSpec((1, LANES), const)],
        out_specs=[pl.BlockSpec((tm, d), lambda i: (i, 0)),
                   pl.BlockSpec((tm * (d // LANES), LANES), lambda i: (i, 0)),
                   pl.BlockSpec((tm, LANES), lambda i: (i, 0))],
        out_shape=[jax.ShapeDtypeStruct((t, d), F32), jax.ShapeDtypeStruct((t * (d // LANES), LANES), F32),
                   jax.ShapeDtypeStruct((t, LANES), F32)],
        compiler_params=_params(("parallel",)),
        name="merge_out_ln1",
    )(u, x2, gi, bi, wo, g1, b1, wr, br)


def _route_kernel(lg_ref, tri_ref, id_ref, gt_ref, cnt_ref, pre_scr):
    x = lg_ref[...]
    lane = lax.broadcasted_iota(jnp.int32, x.shape, 1).astype(F32)
    big = jnp.float32(1 << 20)
    neg = -jnp.inf
    cmask = lane < N_GROUPS
    cm = jnp.max(jnp.where(cmask, x, neg), axis=-1, keepdims=True)
    grp = jnp.min(jnp.where(cmask & (x == cm), lane, big), axis=-1, keepdims=True)
    csum = jnp.sum(jnp.where(cmask, jnp.exp(x - cm), 0.0), axis=-1, keepdims=True)
    p_grp = 1.0 / csum
    lo = N_GROUPS + grp * EXPERTS_PER_GROUP
    fmask = (lane >= lo) & (lane < lo + EXPERTS_PER_GROUP)
    f1 = jnp.max(jnp.where(fmask, x, neg), axis=-1, keepdims=True)
    i1 = jnp.min(jnp.where(fmask & (x == f1), lane, big), axis=-1, keepdims=True)
    mask2 = fmask & (lane != i1)
    f2 = jnp.max(jnp.where(mask2, x, neg), axis=-1, keepdims=True)
    i2 = jnp.min(jnp.where(mask2 & (x == f2), lane, big), axis=-1, keepdims=True)
    e2 = jnp.exp(f2 - f1)
    g1 = p_grp / (1.0 + e2)
    g2 = p_grp * e2 / (1.0 + e2)
    gt_ref[...] = jnp.where(lane == 0, g1, jnp.where(lane == 1, g2, 0.0))

    @pl.when(pl.program_id(0) == 0)
    def _():
        cnt_ref[...] = jnp.zeros_like(cnt_ref)

    sel1 = lane == i1
    sel2 = lane == i2
    chosen = jnp.where(sel1 | sel2, 1.0, 0.0)
    sub = tri_ref.shape[0]
    carry = cnt_ref[...]
    for s in range(x.shape[0] // sub):
        cs = chosen[s * sub:(s + 1) * sub]
        pre_scr[s * sub:(s + 1) * sub, :] = jnp.dot(tri_ref[...], cs.astype(BF16), preferred_element_type=F32) + carry
        carry = carry + jnp.sum(cs, axis=0, keepdims=True)
    cnt_ref[...] = carry
    before = pre_scr[...]
    r1 = jnp.sum(jnp.where(sel1, before, 0.0), axis=-1, keepdims=True)
    r2 = jnp.sum(jnp.where(sel2, before, 0.0), axis=-1, keepdims=True)
    ids = jnp.where(lane == 0, i1 - N_GROUPS, jnp.where(lane == 1, i2 - N_GROUPS,
                    jnp.where(lane == 2, r1, jnp.where(lane == 3, r2, 0.0))))
    id_ref[...] = ids.astype(jnp.int32)


def _route(logits, tm=2048, sub=256):
    t = logits.shape[0]
    spec = pl.BlockSpec((tm, LANES), lambda i: (i, 0))
    tri = jnp.asarray(np.tril(np.ones((sub, sub), np.float32), -1), BF16)
    return pl.pallas_call(
        _route_kernel,
        grid=(t // tm,),
        in_specs=[spec, pl.BlockSpec((sub, sub), lambda i: (0, 0))],
        out_specs=[spec, spec, pl.BlockSpec((1, LANES), lambda i: (0, 0))],
        out_shape=[jax.ShapeDtypeStruct((t, LANES), jnp.int32), jax.ShapeDtypeStruct((t, LANES), F32),
                   jax.ShapeDtypeStruct((1, LANES), F32)],
        scratch_shapes=[pltpu.VMEM((tm, LANES), F32)],
        compiler_params=_params(("arbitrary",)),
        name="route_topk",
    )(logits, tri)


def _invert_kernel(d0_ref, d1_ref, lo_ref, hi_ref, o_ref):
    n_tok = d0_ref.shape[0]

    def zero(i, c):
        o_ref[i] = 0
        return c

    for e in range(lo_ref.shape[0]):
        lax.fori_loop(lo_ref[e], hi_ref[e], zero, 0)

    def put(t, c):
        o_ref[d0_ref[t]] = t
        o_ref[d1_ref[t]] = t
        return c
    lax.fori_loop(0, n_tok, put, 0, unroll=8)


def _invert(dest0, dest1, pad_lo, pad_hi, n_rows):
    return pl.pallas_call(
        _invert_kernel,
        grid_spec=pltpu.PrefetchScalarGridSpec(
            num_scalar_prefetch=4, grid=(1,), in_specs=[],
            out_specs=pl.BlockSpec(memory_space=pltpu.SMEM)),
        out_shape=jax.ShapeDtypeStruct((n_rows,), jnp.int32),
        compiler_params=_params(("arbitrary",)),
        name="dispatch_invert",
    )(dest0, dest1, pad_lo, pad_hi)


def _to_token_tiles(ref, val):
    nt = val.shape[1] // LANES
    for s in range(nt):
        ref[pl.ds(s, val.shape[0], stride=nt), :] = val[:, s * LANES:(s + 1) * LANES]


def _from_token_tiles(ref, n, nt):
    return jnp.concatenate([ref[pl.ds(s, n, stride=nt), :] for s in range(nt)], axis=1)


def _row_gather(src_hbm, idx_ref, base, dst, sem, n_rows, nt):
    for r in range(n_rows):
        tok = pl.multiple_of(idx_ref[base + r] * nt, nt)
        pltpu.make_async_copy(src_hbm.at[pl.ds(tok, nt)], dst.at[pl.ds(r * nt, nt)], sem).start()


def _row_gather_wait(src_hbm, dst, sem, n_rows, nt):
    for r in range(n_rows):
        pltpu.make_async_copy(src_hbm.at[pl.ds(0, nt)], dst.at[pl.ds(r * nt, nt)], sem).wait()


def _moe_kernel(be_ref, nb_ref, tok_ref, h_hbm, wg_ref, wu_ref, wd_ref, o_ref, xbuf, sem):
    i = pl.program_id(0)
    n_used = nb_ref[0]
    slot = i % 2
    nt = wg_ref.shape[1] // LANES
    rows = xbuf.shape[1] // nt

    @pl.when(i == 0)
    def _():
        _row_gather(h_hbm, tok_ref, 0, xbuf.at[0], sem.at[0], rows, nt)

    @pl.when(i < n_used)
    def _():
        _row_gather(h_hbm, tok_ref, (i + 1) * rows, xbuf.at[1 - slot], sem.at[1 - slot], rows, nt)
        _row_gather_wait(h_hbm, xbuf.at[slot], sem.at[slot], rows, nt)
        x = _from_token_tiles(xbuf.at[slot], rows, nt).astype(BF16)
        g = jnp.dot(x, wg_ref[0], preferred_element_type=F32)
        u = jnp.dot(x, wu_ref[0], preferred_element_type=F32)
        hdn = (g * jax.nn.sigmoid(g) * u).astype(BF16)
        _to_token_tiles(o_ref, jnp.dot(hdn, wd_ref[0], preferred_element_type=F32))

    @pl.when(i == n_used)
    def _():
        _row_gather_wait(h_hbm, xbuf.at[slot], sem.at[slot], rows, nt)

    @pl.when(i >= n_used)
    def _():
        o_ref[...] = jnp.zeros_like(o_ref)


def _moe_ffn(blk_expert, n_used, row_tok, h_tiles, wg, wu, wd, rows=MOE_ROWS):
    d, f = wg.shape[1], wg.shape[2]
    nt = d // LANES
    n_blk = blk_expert.shape[0]
    assert row_tok.shape[0] == (n_blk + 1) * rows
    grid_spec = pltpu.PrefetchScalarGridSpec(
        num_scalar_prefetch=3,
        grid=(n_blk,),
        in_specs=[pl.BlockSpec(memory_space=pl.ANY),
                  pl.BlockSpec((1, d, f), lambda i, be, nb, tk: (be[i], 0, 0)),
                  pl.BlockSpec((1, d, f), lambda i, be, nb, tk: (be[i], 0, 0)),
                  pl.BlockSpec((1, f, d), lambda i, be, nb, tk: (be[i], 0, 0))],
        out_specs=pl.BlockSpec((rows * nt, LANES), lambda i, be, nb, tk: (i, 0)),
        scratch_shapes=[pltpu.VMEM((2, rows * nt, LANES), F32), pltpu.SemaphoreType.DMA((2,))],
    )
    return pl.pallas_call(
        _moe_kernel,
        grid_spec=grid_spec,
        out_shape=jax.ShapeDtypeStruct((n_blk * rows * nt, LANES), F32),
        compiler_params=_params(("arbitrary",)),
        name="moe_ffn",
    )(blk_expert, n_used, row_tok, h_tiles, wg, wu, wd)


def _combine_kernel(p0_ref, p1_ref, ys_hbm, h_ref, gt_ref, g2_ref, b2_ref, o_ref, ybuf, sem):
    i = pl.program_id(0)
    n = pl.num_programs(0)
    slot = i % 2
    tm, d = h_ref.shape
    nt = d // LANES

    def issue(step, s):
        _row_gather(ys_hbm, p0_ref, step * tm, ybuf.at[s, 0], sem.at[s], tm, nt)
        _row_gather(ys_hbm, p1_ref, step * tm, ybuf.at[s, 1], sem.at[s], tm, nt)

    def drain(s):
        _row_gather_wait(ys_hbm, ybuf.at[s, 0], sem.at[s], tm, nt)
        _row_gather_wait(ys_hbm, ybuf.at[s, 1], sem.at[s], tm, nt)

    @pl.when(i == 0)
    def _():
        issue(0, 0)

    issue(i + 1, 1 - slot)
    drain(slot)
    gt = gt_ref[...]
    y0 = _from_token_tiles(ybuf.at[slot, 0], tm, nt)
    y1 = _from_token_tiles(ybuf.at[slot, 1], tm, nt)
    moe = gt[:, 0:1] * y0 + gt[:, 1:2] * y1
    o_ref[...] = _layer_norm(DEEPNORM_ALPHA * h_ref[...] + moe, g2_ref[...], b2_ref[...])

    @pl.when(i == n - 1)
    def _():
        drain(1 - slot)


def _combine(pos0, pos1, ys, h1, gates, g2, b2, tm=256):
    t, d = h1.shape
    assert pos0.shape[0] == t + tm and pos1.shape[0] == t + tm
    const = lambda i, a, b: (0, 0)
    grid_spec = pltpu.PrefetchScalarGridSpec(
        num_scalar_prefetch=2,
        grid=(t // tm,),
        in_specs=[pl.BlockSpec(memory_space=pl.ANY),
                  pl.BlockSpec((tm, d), lambda i, a, b: (i, 0)),
                  pl.BlockSpec((tm, LANES), lambda i, a, b: (i, 0)),
                  pl.BlockSpec((1, d), const), pl.BlockSpec((1, d), const)],
        out_specs=pl.BlockSpec((tm, d), lambda i, a, b: (i, 0)),
        scratch_shapes=[pltpu.VMEM((2, 2, tm * (d // LANES), LANES), F32), pltpu.SemaphoreType.DMA((2,))],
    )
    return pl.pallas_call(
        _combine_kernel,
        grid_spec=grid_spec,
        out_shape=jax.ShapeDtypeStruct((t, d), F32),
        compiler_params=_params(("arbitrary",)),
        name="moe_combine_ln2",
    )(pos0, pos1, ys, h1, gates, g2, b2)


def _rope_tables(seq):
    rows = seq // GRID_W
    row_idx = jnp.repeat(jnp.arange(rows, dtype=jnp.int32), GRID_W).astype(F32)
    col_idx = jnp.tile(jnp.arange(GRID_W, dtype=jnp.int32), rows).astype(F32)
    half = HEAD_DIM // 2
    inv_freq = ROPE_THETA ** (-jnp.arange(0, half, 2, dtype=F32) / half)
    ang_r = row_idx[:, None] * inv_freq[None, :]
    ang_c = col_idx[:, None] * inv_freq[None, :]
    cr, sr, cc, sc = jnp.cos(ang_r), jnp.sin(ang_r), jnp.cos(ang_c), jnp.sin(ang_c)
    cs = jnp.concatenate([cr, cc, cr, cc], axis=-1)
    sn = jnp.concatenate([-sr, -sc, sr, sc], axis=-1)
    return cs, sn


def _pair_split(v):
    lead = v.shape[:-1]
    q4 = v.reshape(lead + (-1, 2, 2, HEAD_DIM // 4))
    return jnp.swapaxes(q4, -3, -2).reshape(v.shape)


def _filter_features(seq):
    n1 = 2 * seq // DFT_MINOR
    n = (DFT_MINOR * np.arange(n1)[None, :] + np.arange(DFT_MINOR)[:, None]).astype(np.int64)
    j = np.where(n <= seq, n, 2 * seq - n)
    j = np.where(n == seq, 0, j).astype(np.float64)
    t = (j / (seq - 1))[..., None]
    w = (2.0 * math.pi * j / seq)[..., None]
    bands = np.linspace(1e-4, FILTER_BANDS - 1, FILTER_BANDS, dtype=np.float32).astype(np.float64)
    feats = np.zeros(n.shape + (LANES,), np.float32)
    feats[..., 0:1] = t
    feats[..., 1:1 + FILTER_BANDS] = np.cos(bands * w)
    feats[..., 1 + FILTER_BANDS:FILTER_EMB] = -np.sin(bands * w)
    feats[..., LANES - 1] = (n != seq)
    return jnp.asarray(feats)


def _dispatch_plan(ids, counts, n_tok, rows, tail):
    counts = counts.astype(jnp.int32)
    padded = (counts + rows - 1) // rows * rows
    pad_end = jnp.cumsum(padded)
    pad_start = pad_end - padded
    n_blk = (n_tok * TOP_K + N_EXPERTS * rows) // rows + 1
    blk_start = jnp.arange(n_blk, dtype=jnp.int32) * rows
    blk_expert = jnp.minimum(jnp.sum((pad_end[None, :] <= blk_start[:, None]).astype(jnp.int32), axis=1),
                             N_EXPERTS - 1)
    n_used = (pad_end[-1] // rows).astype(jnp.int32).reshape(1)
    experts = jnp.arange(N_EXPERTS, dtype=jnp.int32)[None, :]

    def dest(e, rank):
        start = jnp.sum(jnp.where(e[:, None] == experts, pad_start[None, :], 0), axis=1)
        return jnp.pad(start + rank, (0, tail))

    pos0 = dest(ids[:, 0], ids[:, 2])
    pos1 = dest(ids[:, 1], ids[:, 3])
    n_map = (n_blk + 1) * rows
    pad_lo = pad_start + counts
    pad_hi = pad_end.at[N_EXPERTS - 1].set(n_map)
    row_tok = _invert(pos0[:n_tok], pos1[:n_tok], pad_lo, pad_hi, n_map)
    return blk_expert, n_used, row_tok, pos0, pos1


def kernel(x, ln_in_g, ln_in_b, w_in, b_gate, q_norm_g, k_norm_g, hy_conv_w, hy_conv_b, filt_w1, filt_b1, filt_f1, filt_w2, filt_b2, filt_f2, filt_w3, hy_bias_d, w_attn_o, w_hy_o, w_out, ln1_g, ln1_b, w_route_grp, b_route_grp, w_route_exp, b_route_exp, w_exp_gate, w_exp_up, w_exp_down, ln2_g, ln2_b):
    batch, seq, d = x.shape
    assert batch == 2, "the long convolution packs exactly two batch rows as one complex signal"
    t = batch * seq
    hw = hy_bias_d.shape[1]
    l = 0
    x2 = x.reshape(t, d)
    row = lambda v: v.reshape(1, -1)

    cs, sn = _rope_tables(seq)
    qg = row(q_norm_g[l]) * (HEAD_DIM ** -0.5 * math.log2(math.e))
    qg = _pair_split(qg)
    kg = _pair_split(row(k_norm_g[l]))
    n_qk = ATTN_WIDTH + KV_WIDTH
    w_bf = jnp.concatenate([_pair_split(w_in[l][:, :n_qk]).astype(BF16), w_in[l][:, n_qk:].astype(BF16)], axis=1)
    proj = _ln_inproj(x2, row(ln_in_g), row(ln_in_b), w_bf, qg, kg, cs, sn, seq)

    attn = _attention(proj, batch, seq)

    hy_col0 = ATTN_WIDTH + 2 * KV_WIDTH
    z, x0c = _hy_pre(proj, hy_conv_w[l], row(hy_conv_b[l]), batch, seq, hw, hy_col0)
    n1 = 2 * seq // DFT_MINOR
    tab_sig, tab_out, tab_filt, mf, mi = _dft_tables(n1, n1 // 2)
    feats = _filter_features(seq)
    w1p = jnp.pad(filt_w1[l], ((0, LANES - FILTER_EMB), (0, 0)))
    min_decay = math.log(DECAY_TARGET) / SLOW_DECAY_PCT
    max_decay = math.log(DECAY_TARGET) / FAST_DECAY_PCT
    deltas = jnp.linspace(min_decay, max_decay, hw, dtype=F32)[None, :]
    two_sided, sumsq = _filter_time(feats, w1p, row(filt_b1[l]), row(filt_f1[l]), filt_w2[l], row(filt_b2[l]),
                                    row(filt_f2[l]), filt_w3[l], deltas)
    fa = _dft_stage_a(two_sided, tab_filt, 2 * n1, t2_major=True)
    hspec = _filter_stage_b(fa, mf, sumsq, n1)
    za = _dft_stage_a(z.reshape(batch, n1 // 2, DFT_MINOR * hw), tab_sig, 2 * n1)
    zb = _conv_stage_b(za, mf, mi, hspec, n1)
    yconv = _conv_stage_out(zb, tab_out, batch, n1).reshape(t, hw)

    gate_col0 = hy_col0 + 3 * hw
    u = _merge_a(attn, yconv, z, x0c, row(hy_bias_d[l]), proj, row(b_gate[l]), w_attn_o[l].astype(BF16),
                 w_hy_o[l].astype(BF16), gate_col0)
    n_r = N_GROUPS + N_EXPERTS
    wr = jnp.pad(jnp.concatenate([w_route_grp[l], w_route_exp[l]], axis=1), ((0, 0), (0, LANES - n_r)))
    br = jnp.pad(jnp.concatenate([b_route_grp[l], b_route_exp[l]]), (0, LANES - n_r)).reshape(1, LANES)
    h1, h1_tiles, logits = _merge_b(u, x2, row(ln_in_g), row(ln_in_b), w_out[l].astype(BF16), row(ln1_g[l]), row(ln1_b[l]),
                          wr, br)

    ids, gates, cnt = _route(logits)
    counts = cnt[0, N_GROUPS:N_GROUPS + N_EXPERTS]
    combine_tile = 256
    blk_expert, n_used, row_tok, pos0, pos1 = _dispatch_plan(ids, counts, t, MOE_ROWS, combine_tile)
    ys = _moe_ffn(blk_expert, n_used, row_tok, h1_tiles, w_exp_gate[l].astype(BF16), w_exp_up[l].astype(BF16),
                  w_exp_down[l].astype(BF16))
    out = _combine(pos0, pos1, ys, h1, gates, row(ln2_g[l]), row(ln2_b[l]), tm=combine_tile)
    return out.reshape(batch, seq, d)
```

```python
import functools
import math

import jax
import jax.numpy as jnp
import numpy as np
from jax import lax
from jax.experimental import pallas as pl
from jax.experimental.pallas import tpu as pltpu

F32 = jnp.float32
BF16 = jnp.bfloat16

GRID_W = 64
N_Q_HEADS = 8
N_KV_HEADS = 2
HEAD_DIM = 128
Q_GROUP = N_Q_HEADS // N_KV_HEADS
ATTN_WIDTH = N_Q_HEADS * HEAD_DIM
KV_WIDTH = N_KV_HEADS * HEAD_DIM
ROPE_THETA = 10000.0
QK_EPS = 1e-6
SHORT_CONV = 3
FILTER_BANDS = 16
FILTER_EMB = 1 + 2 * FILTER_BANDS
DECAY_TARGET = 1e-2
FAST_DECAY_PCT = 0.3
SLOW_DECAY_PCT = 1.5
FILTER_EPS = 1e-6
N_GROUPS = 4
EXPERTS_PER_GROUP = 8
N_EXPERTS = N_GROUPS * EXPERTS_PER_GROUP
TOP_K = 2
LN_EPS = 1e-5
DEPTH = 1
DEEPNORM_ALPHA = (2 * DEPTH) ** 0.25

LANES = 128
V7X_VMEM_LIMIT = 56 * 1024 * 1024
DFT_MINOR = 128

MOE_ROWS = 256


def _params(sem, vmem=V7X_VMEM_LIMIT):
    return pltpu.CompilerParams(dimension_semantics=sem, vmem_limit_bytes=vmem)


def _layer_norm(x, g, b):
    mu = jnp.mean(x, axis=-1, keepdims=True)
    xc = x - mu
    var = jnp.mean(xc * xc, axis=-1, keepdims=True)
    return xc * lax.rsqrt(var + LN_EPS) * g + b


def _norm_rope(acc, gain, cs, sn):
    ms = jnp.mean(acc * acc, axis=-1, keepdims=True)
    xn = acc * lax.rsqrt(ms + QK_EPS) * gain
    return xn * cs + pltpu.roll(xn, HEAD_DIM // 2, 1) * sn


def _ln_inproj_kernel(x_ref, g_ref, b_ref, w_ref, qg_ref, kg_ref, cs_ref, sn_ref, o_ref, h_scr, *, tn):
    j = pl.program_id(1)
    nchunk = tn // HEAD_DIM
    n_q = ATTN_WIDTH // HEAD_DIM
    n_qk = n_q + N_KV_HEADS

    @pl.when(j == 0)
    def _():
        h = _layer_norm(x_ref[...], g_ref[...], b_ref[...])
        h_scr[...] = h.astype(BF16)

    acc = jnp.dot(h_scr[...], w_ref[...], preferred_element_type=F32)

    def store(first_plain_chunk_fn):
        for c in range(nchunk):
            sl = slice(c * HEAD_DIM, (c + 1) * HEAD_DIM)
            kind = first_plain_chunk_fn(c)
            if kind == "q":
                o_ref[:, sl] = _norm_rope(acc[:, sl], qg_ref[...], cs_ref[...], sn_ref[...]).astype(BF16)
            elif kind == "k":
                o_ref[:, sl] = _norm_rope(acc[:, sl], kg_ref[...], cs_ref[...], sn_ref[...]).astype(BF16)
            else:
                o_ref[:, sl] = acc[:, sl].astype(BF16)

    n_q_tiles = n_q // nchunk
    assert n_q % nchunk == 0 and N_KV_HEADS <= nchunk

    @pl.when(j < n_q_tiles)
    def _():
        store(lambda c: "q")

    @pl.when(j == n_q_tiles)
    def _():
        store(lambda c: "k" if c < N_KV_HEADS else "p")

    @pl.when(j > n_q_tiles)
    def _():
        store(lambda c: "p")


def _ln_inproj(x2, g, b, w_bf, qg, kg, cs, sn, seq, tm=1024, tn=512):
    t, d = x2.shape
    n = w_bf.shape[1]
    assert t % tm == 0 and n % tn == 0 and seq % tm == 0
    pos_blocks = seq // tm
    return pl.pallas_call(
        functools.partial(_ln_inproj_kernel, tn=tn),
        grid=(t // tm, n // tn),
        in_specs=[
            pl.BlockSpec((tm, d), lambda i, j: (i, 0)),
            pl.BlockSpec((1, d), lambda i, j: (0, 0)),
            pl.BlockSpec((1, d), lambda i, j: (0, 0)),
            pl.BlockSpec((d, tn), lambda i, j: (0, j)),
            pl.BlockSpec((1, HEAD_DIM), lambda i, j: (0, 0)),
            pl.BlockSpec((1, HEAD_DIM), lambda i, j: (0, 0)),
            pl.BlockSpec((tm, HEAD_DIM), lambda i, j: (i % pos_blocks, 0)),
            pl.BlockSpec((tm, HEAD_DIM), lambda i, j: (i % pos_blocks, 0)),
        ],
        out_specs=pl.BlockSpec((tm, tn), lambda i, j: (i, j)),
        out_shape=jax.ShapeDtypeStruct((t, n), BF16),
        scratch_shapes=[pltpu.VMEM((tm, d), BF16)],
        compiler_params=_params(("parallel", "arbitrary")),
        name="ln_inproj",
    )(x2, g, b, w_bf, qg, kg, cs, sn)


def _attn_kernel(q_ref, k_ref, v_ref, o_ref, vt_scr, *, tq, tk):
    seq = k_ref.shape[0]
    n_chunks = seq // tk
    m_cols = Q_GROUP * tq

    @pl.when(pl.program_id(2) == 0)
    def _():
        for c in range(n_chunks):
            vt_scr[c] = v_ref[c * tk:(c + 1) * tk, :].T

    qt = jnp.concatenate([q_ref[:, h * HEAD_DIM:(h + 1) * HEAD_DIM].T for h in range(Q_GROUP)], axis=1)

    def scores(c):
        start = pl.multiple_of(c * tk, tk)
        return jnp.dot(k_ref[pl.ds(start, tk), :], qt, preferred_element_type=F32)

    def update(c, s, m, l, acc):
        m_new = jnp.maximum(m, jnp.max(s, axis=0, keepdims=True))
        p = jnp.exp2(s - m_new)
        alpha = jnp.exp2(m - m_new)
        l_new = alpha * l + jnp.sum(p, axis=0, keepdims=True)
        acc_new = alpha * acc + jnp.dot(vt_scr[c], p.astype(BF16), preferred_element_type=F32)
        return m_new, l_new, acc_new

    def body(c, carry):
        m, l, acc, s = carry
        s_next = scores(jnp.minimum(c + 1, n_chunks - 1))
        m, l, acc = update(c, s, m, l, acc)
        return m, l, acc, s_next

    m0 = jnp.full((1, m_cols), -jnp.inf, F32)
    l0 = jnp.zeros((1, m_cols), F32)
    a0 = jnp.zeros((HEAD_DIM, m_cols), F32)
    _, l, acc, _ = lax.fori_loop(0, n_chunks, body, (m0, l0, a0, scores(0)), unroll=8)
    out = acc / l
    for h in range(Q_GROUP):
        o_ref[:, h * HEAD_DIM:(h + 1) * HEAD_DIM] = out[:, h * tq:(h + 1) * tq].T.astype(BF16)


def _attention(proj, batch, seq, tq=128, tk=512):
    t = proj.shape[0]
    nq = seq // tq
    gw = Q_GROUP * HEAD_DIM
    k_col0 = ATTN_WIDTH // HEAD_DIM
    v_col0 = (ATTN_WIDTH + KV_WIDTH) // HEAD_DIM
    return pl.pallas_call(
        functools.partial(_attn_kernel, tq=tq, tk=tk),
        grid=(batch, N_KV_HEADS, nq),
        in_specs=[
            pl.BlockSpec((tq, gw), lambda b, g, i: (b * nq + i, g)),
            pl.BlockSpec((seq, HEAD_DIM), lambda b, g, i: (b, k_col0 + g)),
            pl.BlockSpec((seq, HEAD_DIM), lambda b, g, i: (b, v_col0 + g)),
        ],
        out_specs=pl.BlockSpec((tq, gw), lambda b, g, i: (b * nq + i, g)),
        out_shape=jax.ShapeDtypeStruct((t, ATTN_WIDTH), BF16),
        scratch_shapes=[pltpu.VMEM((seq // tk, HEAD_DIM, tk), BF16)],
        compiler_params=_params(("parallel", "parallel", "arbitrary")),
        name="attention",
    )(proj, proj, proj)


def _hy_pre_kernel(x0_ref, x1_ref, hv_ref,
                   p0_ref, p1_ref, pv_ref, n0_ref, n1_ref, nv_ref,
                   w0_ref, w1_ref, wv_ref, b0_ref, b1_ref, bv_ref,
                   z_ref, x0c_ref):
    r = pl.program_id(2)
    last = pl.num_programs(2) - 1
    ts = x0_ref.shape[0]
    row = lax.broadcasted_iota(jnp.int32, x0_ref.shape, 0)

    def conv(x_ref, p_ref, n_ref, w_ref, b_ref):
        x = x_ref[...].astype(F32)
        prev_row = jnp.where(r == 0, 0.0, p_ref[7:8, :].astype(F32))
        next_row = jnp.where(r == last, 0.0, n_ref[0:1, :].astype(F32))
        up = jnp.where(row == 0, prev_row, pltpu.roll(x, 1, 0))
        dn = jnp.where(row == ts - 1, next_row, pltpu.roll(x, ts - 1, 0))
        w = w_ref[...]
        return b_ref[...] + up * w[0:1, :] + x * w[1:2, :] + dn * w[2:3, :]

    x0c = conv(x0_ref, p0_ref, n0_ref, w0_ref, b0_ref)
    x1c = conv(x1_ref, p1_ref, n1_ref, w1_ref, b1_ref)
    hvc = conv(hv_ref, pv_ref, nv_ref, wv_ref, bv_ref)
    z_ref[...] = (hvc * x1c).astype(BF16)
    x0c_ref[...] = x0c.astype(BF16)


def _hy_pre(proj, conv_w, conv_b, batch, seq, hw, col0, ts=512, tc=256):
    t = proj.shape[0]
    nr = seq // ts
    nct = hw // tc
    cb0 = col0 // tc
    halo = 8
    hb = ts // halo

    def main(off):
        return pl.BlockSpec((ts, tc), lambda b, c, r: (b * nr + r, cb0 + off * nct + c))

    def prev(off):
        return pl.BlockSpec((halo, tc), lambda b, c, r: (jnp.maximum((b * nr + r) * hb - 1, 0), cb0 + off * nct + c))

    def nxt(off):
        return pl.BlockSpec((halo, tc), lambda b, c, r: (jnp.minimum((b * nr + r + 1) * hb, t // halo - 1),
                                                       cb0 + off * nct + c))

    def wspec(off):
        return pl.BlockSpec((SHORT_CONV, tc), lambda b, c, r: (0, off * nct + c))

    def bspec(off):
        return pl.BlockSpec((1, tc), lambda b, c, r: (0, off * nct + c))

    out_spec = pl.BlockSpec((ts, tc), lambda b, c, r: (b * nr + r, c))
    return pl.pallas_call(
        _hy_pre_kernel,
        grid=(batch, nct, nr),
        in_specs=[main(0), main(1), main(2), prev(0), prev(1), prev(2), nxt(0), nxt(1), nxt(2),
                  wspec(0), wspec(1), wspec(2), bspec(0), bspec(1), bspec(2)],
        out_specs=[out_spec, out_spec],
        out_shape=[jax.ShapeDtypeStruct((t, hw), BF16), jax.ShapeDtypeStruct((t, hw), BF16)],
        compiler_params=_params(("parallel", "parallel", "parallel")),
        name="hy_pre",
    )(proj, proj, proj, proj, proj, proj, proj, proj, proj,
      conv_w, conv_w, conv_w, conv_b, conv_b, conv_b)


def _dot3(a, b):
    a_hi = a.astype(BF16)
    a_lo = (a - a_hi.astype(F32)).astype(BF16)
    b_hi = b.astype(BF16)
    b_lo = (b - b_hi.astype(F32)).astype(BF16)
    out = jnp.dot(a_hi, b_hi, preferred_element_type=F32)
    out += jnp.dot(a_hi, b_lo, preferred_element_type=F32)
    out += jnp.dot(a_lo, b_hi, preferred_element_type=F32)
    return out


def _filter_kernel(feat_ref, w1_ref, b1_ref, f1_ref, w2_ref, b2_ref, f2_ref, w3_ref, dl_ref, o_ref, ss_ref):
    feats = feat_ref[...]
    feats = feats.reshape(-1, feats.shape[-1])
    h = jnp.sin(f1_ref[...] * (_dot3(feats, w1_ref[...]) + b1_ref[...]))
    h = jnp.sin(f2_ref[...] * (_dot3(h, w2_ref[...]) + b2_ref[...]))
    filt = jnp.dot(h.astype(BF16), w3_ref[...].astype(BF16), preferred_element_type=F32)
    tpos = feats[:, 0:1]
    live = feats[:, LANES - 1:LANES]
    val = filt * jnp.exp(-tpos * jnp.abs(dl_ref[...])) * live

    @pl.when((pl.program_id(0) == 0) & (pl.program_id(1) == 0))
    def _():
        ss_ref[...] = jnp.zeros_like(ss_ref)

    ss_ref[...] += jnp.sum(val * val, axis=0, keepdims=True)
    o_ref[...] = val.reshape(o_ref.shape).astype(BF16)


def _filter_time(feats, w1p, b1, f1, w2, b2, f2, w3, deltas, t2_blk=8):
    _, n1, _ = feats.shape
    c = w3.shape[1] // 2
    hid = w2.shape[0]
    const = lambda i, h: (0, 0)
    return pl.pallas_call(
        _filter_kernel,
        grid=(DFT_MINOR // t2_blk, 2),
        in_specs=[
            pl.BlockSpec((t2_blk, n1 // 2, LANES), lambda i, h: (i, h, 0)),
            pl.BlockSpec((LANES, hid), const),
            pl.BlockSpec((1, hid), const),
            pl.BlockSpec((1, hid), const),
            pl.BlockSpec((hid, hid), const),
            pl.BlockSpec((1, hid), const),
            pl.BlockSpec((1, hid), const),
            pl.BlockSpec((hid, c), lambda i, h: (0, h)),
            pl.BlockSpec((1, c), const),
        ],
        out_specs=[pl.BlockSpec((t2_blk, n1 // 2, c), lambda i, h: (i, h, 0)), pl.BlockSpec((1, c), const)],
        out_shape=[jax.ShapeDtypeStruct((DFT_MINOR, n1, c), BF16), jax.ShapeDtypeStruct((1, c), F32)],
        compiler_params=_params(("arbitrary", "arbitrary")),
        name="filter_time",
    )(feats, w1p, b1, f1, w2, b2, f2, w3, deltas)


def _dft_a_kernel(x_ref, tab_ref, o_ref):
    x = x_ref[...]
    x = x.reshape(-1, x.shape[-1])
    o_ref[0] = jnp.dot(tab_ref[0], x, preferred_element_type=F32).astype(o_ref.dtype)


def _dft_stage_a(x3, tab, n_out_rows, t2_major=False):
    if t2_major:
        _, r, c = x3.shape
        g = 1
        x_spec = pl.BlockSpec((1, r, c), lambda t2: (t2, 0, 0))
    else:
        g, r, wc = x3.shape
        c = wc // DFT_MINOR
        x_spec = pl.BlockSpec((g, r, c), lambda t2: (0, 0, t2))
    return pl.pallas_call(
        _dft_a_kernel,
        grid=(DFT_MINOR,),
        in_specs=[x_spec,
                  pl.BlockSpec((1, n_out_rows, g * r), lambda t2: (t2, 0, 0))],
        out_specs=pl.BlockSpec((1, n_out_rows, c), lambda t2: (t2, 0, 0)),
        out_shape=jax.ShapeDtypeStruct((DFT_MINOR, n_out_rows, c), BF16),
        compiler_params=_params(("parallel",)),
        name="dft_stage_a",
    )(x3, tab)


def _filt_b_kernel(re_ref, im_ref, mf_ref, ss_ref, o_ref, *, n_total):
    rhs = jnp.concatenate([re_ref[...], im_ref[...]], axis=0)
    spec = jnp.dot(mf_ref[...], rhs, preferred_element_type=F32)
    scale = lax.rsqrt(ss_ref[...] + FILTER_EPS) * (1.0 / n_total)
    o_ref[0] = (spec * scale).astype(o_ref.dtype)


def _filter_stage_b(a_arr, mf, sumsq, n1):
    _, rows, c = a_arr.shape
    a2 = a_arr.reshape(DFT_MINOR, rows * c)
    n_total = n1 * DFT_MINOR
    return pl.pallas_call(
        functools.partial(_filt_b_kernel, n_total=n_total),
        grid=(n1,),
        in_specs=[pl.BlockSpec((DFT_MINOR, c), lambda k1: (0, k1)),
                  pl.BlockSpec((DFT_MINOR, c), lambda k1: (0, n1 + k1)),
                  pl.BlockSpec((2 * DFT_MINOR, 2 * DFT_MINOR), lambda k1: (0, 0)),
                  pl.BlockSpec((1, c), lambda k1: (0, 0))],
        out_specs=pl.BlockSpec((1, 2 * DFT_MINOR, c), lambda k1: (k1, 0, 0)),
        out_shape=jax.ShapeDtypeStruct((n1, 2 * DFT_MINOR, c), BF16),
        compiler_params=_params(("parallel",)),
        name="filter_stage_b",
    )(a2, a2, mf, sumsq)


def _conv_b_kernel(re_ref, im_ref, mf_ref, mi_ref, h_ref, o_ref):
    half = DFT_MINOR
    rhs = jnp.concatenate([re_ref[...], im_ref[...]], axis=0)
    spec = jnp.dot(mf_ref[...], rhs, preferred_element_type=F32)
    hh = h_ref[0].astype(F32)
    xr, xi = spec[:half], spec[half:]
    hr, hi = hh[:half], hh[half:]
    yr = xr * hr - xi * hi
    yi = xr * hi + xi * hr
    y = jnp.concatenate([yr, yi], axis=0).astype(BF16)
    o_ref[0] = jnp.dot(mi_ref[...], y, preferred_element_type=F32).astype(o_ref.dtype)


def _conv_stage_b(a_arr, mf, mi, hspec, n1):
    _, rows, c = a_arr.shape
    a2 = a_arr.reshape(DFT_MINOR, rows * c)
    return pl.pallas_call(
        _conv_b_kernel,
        grid=(n1,),
        in_specs=[pl.BlockSpec((DFT_MINOR, c), lambda k1: (0, k1)),
                  pl.BlockSpec((DFT_MINOR, c), lambda k1: (0, n1 + k1)),
                  pl.BlockSpec((2 * DFT_MINOR, 2 * DFT_MINOR), lambda k1: (0, 0)),
                  pl.BlockSpec((2 * DFT_MINOR, 2 * DFT_MINOR), lambda k1: (0, 0)),
                  pl.BlockSpec((1, 2 * DFT_MINOR, c), lambda k1: (k1, 0, 0))],
        out_specs=pl.BlockSpec((1, 2 * DFT_MINOR, c), lambda k1: (k1, 0, 0)),
        out_shape=jax.ShapeDtypeStruct((n1, 2 * DFT_MINOR, c), BF16),
        compiler_params=_params(("parallel",)),
        name="conv_stage_b",
    )(a2, a2, mf, mi, hspec)


def _conv_out_kernel(re_ref, im_ref, tab_ref, o_ref):
    rhs = jnp.concatenate([re_ref[...], im_ref[...]], axis=0)
    y = jnp.dot(tab_ref[0], rhs, preferred_element_type=F32)
    o_ref[...] = y.reshape(o_ref.shape).astype(o_ref.dtype)


def _conv_stage_out(b_arr, tab, batch, n1):
    _, rows, c = b_arr.shape
    b2 = b_arr.reshape(n1, rows * c)
    r = n1 // 2
    return pl.pallas_call(
        _conv_out_kernel,
        grid=(DFT_MINOR,),
        in_specs=[pl.BlockSpec((n1, c), lambda t2: (0, t2)),
                  pl.BlockSpec((n1, c), lambda t2: (0, DFT_MINOR + t2)),
                  pl.BlockSpec((1, batch * r, 2 * n1), lambda t2: (t2, 0, 0))],
        out_specs=pl.BlockSpec((batch, r, c), lambda t2: (0, 0, t2)),
        out_shape=jax.ShapeDtypeStruct((batch, r, DFT_MINOR * c), BF16),
        compiler_params=_params(("parallel",)),
        name="conv_stage_out",
    )(b2, b2, tab)


def _dft_tables(n1, n_sig_rows):
    n = n1 * DFT_MINOR
    k1 = np.arange(n1, dtype=np.int64)[None, :, None]
    t2 = np.arange(DFT_MINOR, dtype=np.int64)[:, None, None]

    def twiddled(n_t1):
        t1 = np.arange(n_t1, dtype=np.int64)[None, None, :]
        m = (k1 * (DFT_MINOR * t1 + t2)) % n
        ang = m.astype(np.float64) * (-2.0 * math.pi / n)
        return np.cos(ang).astype(np.float32), np.sin(ang).astype(np.float32)

    pr, pi = twiddled(n_sig_rows)
    tab_sig = np.concatenate([np.concatenate([pr, -pi], axis=2), np.concatenate([pi, pr], axis=2)], axis=1)
    prt, pit = np.swapaxes(pr, 1, 2), np.swapaxes(pi, 1, 2)
    tab_out = np.concatenate([np.concatenate([prt, pit], axis=2), np.concatenate([-pit, prt], axis=2)], axis=1)
    fr, fi = twiddled(n1)
    tab_filt = np.concatenate([fr, fi], axis=1)
    j = np.arange(DFT_MINOR)
    ang = -2.0 * np.pi * ((j[:, None] * j[None, :]) % DFT_MINOR) / DFT_MINOR
    cr, ci = np.cos(ang), np.sin(ang)
    mf = np.block([[cr, -ci], [ci, cr]])
    mi = np.block([[cr, ci], [-ci, cr]])
    return tuple(jnp.asarray(a, BF16) for a in (tab_sig, tab_out, tab_filt, mf, mi))


def _merge_a_kernel(a_ref, yc_ref, z_ref, x0_ref, d_ref, ga_ref, gh_ref, bga_ref, bgh_ref, wa_ref, wh_ref, o_ref, hy_scr):
    @pl.when(pl.program_id(1) == 0)
    def _():
        z = z_ref[...].astype(F32)
        hy = x0_ref[...].astype(F32) * (yc_ref[...].astype(F32) + z * d_ref[...])
        hy_scr[...] = hy.astype(BF16)

    ya = jnp.dot(a_ref[...], wa_ref[...], preferred_element_type=F32)
    yh = jnp.dot(hy_scr[...], wh_ref[...], preferred_element_type=F32)
    ga = jax.nn.sigmoid(ga_ref[...].astype(F32) + bga_ref[...])
    gh = jax.nn.sigmoid(gh_ref[...].astype(F32) + bgh_ref[...])
    o_ref[...] = (ga * ya + gh * yh).astype(BF16)


def _merge_a(attn, yconv, z, x0c, d_bias, proj, b_gate2, wa, wh, gate_col0, tm=512, tn=512):
    t, aw = attn.shape
    hw = yconv.shape[1]
    d = wa.shape[1]
    assert gate_col0 % tn == 0 and d % tn == 0
    gb = gate_col0 // tn
    nj = d // tn
    hy_spec = pl.BlockSpec((tm, hw), lambda i, j: (i, 0))
    return pl.pallas_call(
        _merge_a_kernel,
        grid=(t // tm, nj),
        in_specs=[pl.BlockSpec((tm, aw), lambda i, j: (i, 0)),
                  hy_spec, hy_spec, hy_spec,
                  pl.BlockSpec((1, hw), lambda i, j: (0, 0)),
                  pl.BlockSpec((tm, tn), lambda i, j: (i, gb + j)),
                  pl.BlockSpec((tm, tn), lambda i, j: (i, gb + nj + j)),
                  pl.BlockSpec((1, tn), lambda i, j: (0, j)),
                  pl.BlockSpec((1, tn), lambda i, j: (0, nj + j)),
                  pl.BlockSpec((aw, tn), lambda i, j: (0, j)),
                  pl.BlockSpec((hw, tn), lambda i, j: (0, j))],
        out_specs=pl.BlockSpec((tm, tn), lambda i, j: (i, j)),
        out_shape=jax.ShapeDtypeStruct((t, d), BF16),
        scratch_shapes=[pltpu.VMEM((tm, hw), BF16)],
        compiler_params=_params(("parallel", "arbitrary")),
        name="merge_gated",
    )(attn, yconv, z, x0c, d_bias, proj, proj, b_gate2, b_gate2, wa, wh)


def _merge_b_kernel(u_ref, x_ref, gi_ref, bi_ref, wo_ref, g1_ref, b1_ref, wr_ref, br_ref, h_ref, lg_ref):
    mixed = jnp.dot(u_ref[...], wo_ref[...], preferred_element_type=F32)
    h0 = _layer_norm(x_ref[...], gi_ref[...], bi_ref[...])
    h1 = _layer_norm(DEEPNORM_ALPHA * h0 + mixed, g1_ref[...], b1_ref[...])
    h_ref[...] = h1
    wr = wr_ref[...]
    h_hi = h1.astype(BF16)
    h_lo = (h1 - h_hi.astype(F32)).astype(BF16)
    w_hi = wr.astype(BF16)
    w_lo = (wr - w_hi.astype(F32)).astype(BF16)
    lg = jnp.dot(h_hi, w_hi, preferred_element_type=F32)
    lg += jnp.dot(h_hi, w_lo, preferred_element_type=F32)
    lg += jnp.dot(h_lo, w_hi, preferred_element_type=F32)
    lg_ref[...] = lg + br_ref[...]


def _merge_b(u, x2, gi, bi, wo, g1, b1, wr, br, tm=512):
    t, d = x2.shape
    const = lambda i: (0, 0)
    return pl.pallas_call(
        _merge_b_kernel,
        grid=(t // tm,),
        in_specs=[pl.BlockSpec((tm, d), lambda i: (i, 0)),
                  pl.BlockSpec((tm, d), lambda i: (i, 0)),
                  pl.BlockSpec((1, d), const), pl.BlockSpec((1, d), const),
                  pl.BlockSpec((d, d), const),
                  pl.BlockSpec((1, d), const), pl.BlockSpec((1, d), const),
                  pl.BlockSpec((d, LANES), const), pl.BlockSpec((1, LANES), const)],
        out_specs=[pl.BlockSpec((tm, d), lambda i: (i, 0)), pl.BlockSpec((tm, LANES), lambda i: (i, 0))],
        out_shape=[jax.ShapeDtypeStruct((t, d), F32), jax.ShapeDtypeStruct((t, LANES), F32)],
        compiler_params=_params(("parallel",)),
        name="merge_out_ln1",
    )(u, x2, gi, bi, wo, g1, b1, wr, br)


def _route_kernel(lg_ref, tri_ref, id_ref, gt_ref, cnt_ref, pre_scr):
    x = lg_ref[...]
    lane = lax.broadcasted_iota(jnp.int32, x.shape, 1).astype(F32)
    big = jnp.float32(1 << 20)
    neg = -jnp.inf
    cmask = lane < N_GROUPS
    cm = jnp.max(jnp.where(cmask, x, neg), axis=-1, keepdims=True)
    grp = jnp.min(jnp.where(cmask & (x == cm), lane, big), axis=-1, keepdims=True)
    csum = jnp.sum(jnp.where(cmask, jnp.exp(x - cm), 0.0), axis=-1, keepdims=True)
    p_grp = 1.0 / csum
    lo = N_GROUPS + grp * EXPERTS_PER_GROUP
    fmask = (lane >= lo) & (lane < lo + EXPERTS_PER_GROUP)
    f1 = jnp.max(jnp.where(fmask, x, neg), axis=-1, keepdims=True)
    i1 = jnp.min(jnp.where(fmask & (x == f1), lane, big), axis=-1, keepdims=True)
    mask2 = fmask & (lane != i1)
    f2 = jnp.max(jnp.where(mask2, x, neg), axis=-1, keepdims=True)
    i2 = jnp.min(jnp.where(mask2 & (x == f2), lane, big), axis=-1, keepdims=True)
    e2 = jnp.exp(f2 - f1)
    g1 = p_grp / (1.0 + e2)
    g2 = p_grp * e2 / (1.0 + e2)
    gt_ref[...] = jnp.where(lane == 0, g1, jnp.where(lane == 1, g2, 0.0))

    @pl.when(pl.program_id(0) == 0)
    def _():
        cnt_ref[...] = jnp.zeros_like(cnt_ref)

    sel1 = lane == i1
    sel2 = lane == i2
    chosen = jnp.where(sel1 | sel2, 1.0, 0.0)
    sub = tri_ref.shape[0]
    carry = cnt_ref[...]
    for s in range(x.shape[0] // sub):
        cs = chosen[s * sub:(s + 1) * sub]
        pre_scr[s * sub:(s + 1) * sub, :] = jnp.dot(tri_ref[...], cs.astype(BF16), preferred_element_type=F32) + carry
        carry = carry + jnp.sum(cs, axis=0, keepdims=True)
    cnt_ref[...] = carry
    before = pre_scr[...]
    r1 = jnp.sum(jnp.where(sel1, before, 0.0), axis=-1, keepdims=True)
    r2 = jnp.sum(jnp.where(sel2, before, 0.0), axis=-1, keepdims=True)
    ids = jnp.where(lane == 0, i1 - N_GROUPS, jnp.where(lane == 1, i2 - N_GROUPS,
                    jnp.where(lane == 2, r1, jnp.where(lane == 3, r2, 0.0))))
    id_ref[...] = ids.astype(jnp.int32)


def _route(logits, tm=2048, sub=256):
    t = logits.shape[0]
    spec = pl.BlockSpec((tm, LANES), lambda i: (i, 0))
    tri = jnp.asarray(np.tril(np.ones((sub, sub), np.float32), -1), BF16)
    return pl.pallas_call(
        _route_kernel,
        grid=(t // tm,),
        in_specs=[spec, pl.BlockSpec((sub, sub), lambda i: (0, 0))],
        out_specs=[spec, spec, pl.BlockSpec((1, LANES), lambda i: (0, 0))],
        out_shape=[jax.ShapeDtypeStruct((t, LANES), jnp.int32), jax.ShapeDtypeStruct((t, LANES), F32),
                   jax.ShapeDtypeStruct((1, LANES), F32)],
        scratch_shapes=[pltpu.VMEM((tm, LANES), F32)],
        compiler_params=_params(("arbitrary",)),
        name="route_topk",
    )(logits, tri)


def _invert_kernel(d0_ref, d1_ref, lo_ref, hi_ref, o_ref):
    n_tok = d0_ref.shape[0]

    def zero(i, c):
        o_ref[i] = 0
        return c

    for e in range(lo_ref.shape[0]):
        lax.fori_loop(lo_ref[e], hi_ref[e], zero, 0)

    def put(t, c):
        o_ref[d0_ref[t]] = t
        o_ref[d1_ref[t]] = t
        return c
    lax.fori_loop(0, n_tok, put, 0, unroll=8)


def _invert(dest0, dest1, pad_lo, pad_hi, n_rows):
    return pl.pallas_call(
        _invert_kernel,
        grid_spec=pltpu.PrefetchScalarGridSpec(
            num_scalar_prefetch=4, grid=(1,), in_specs=[],
            out_specs=pl.BlockSpec(memory_space=pltpu.SMEM)),
        out_shape=jax.ShapeDtypeStruct((n_rows,), jnp.int32),
        compiler_params=_params(("arbitrary",)),
        name="dispatch_invert",
    )(dest0, dest1, pad_lo, pad_hi)


def _row_gather(src_hbm, idx_ref, base, dst, sem, n_rows):
    for r in range(n_rows):
        tok = idx_ref[base + r]
        pltpu.make_async_copy(src_hbm.at[pl.ds(tok, 1)], dst.at[pl.ds(r, 1)], sem).start(priority=r % 2)


def _row_gather_wait(src_hbm, dst, sem, n_rows):
    for r in range(n_rows):
        pltpu.make_async_copy(src_hbm.at[pl.ds(0, 1)], dst.at[pl.ds(r, 1)], sem).wait()


def _moe_kernel(be_ref, nb_ref, tok_ref, h_hbm, wg_ref, wu_ref, wd_ref, o_ref, xbuf, sem):
    i = pl.program_id(0)
    n_used = nb_ref[0]
    slot = i % 2
    rows = xbuf.shape[1]

    @pl.when(i == 0)
    def _():
        _row_gather(h_hbm, tok_ref, 0, xbuf.at[0], sem.at[0], rows)

    @pl.when(i < n_used)
    def _():
        _row_gather(h_hbm, tok_ref, (i + 1) * rows, xbuf.at[1 - slot], sem.at[1 - slot], rows)
        _row_gather_wait(h_hbm, xbuf.at[slot], sem.at[slot], rows)
        x = xbuf[slot].astype(BF16)
        g = jnp.dot(x, wg_ref[0], preferred_element_type=F32)
        u = jnp.dot(x, wu_ref[0], preferred_element_type=F32)
        hdn = (g * jax.nn.sigmoid(g) * u).astype(BF16)
        o_ref[...] = jnp.dot(hdn, wd_ref[0], preferred_element_type=F32)

    @pl.when(i == n_used)
    def _():
        _row_gather_wait(h_hbm, xbuf.at[slot], sem.at[slot], rows)

    @pl.when(i >= n_used)
    def _():
        o_ref[...] = jnp.zeros_like(o_ref)


def _moe_ffn(blk_expert, n_used, row_tok, h1, wg, wu, wd, rows=MOE_ROWS):
    t, d = h1.shape
    f = wg.shape[2]
    n_blk = blk_expert.shape[0]
    assert row_tok.shape[0] == (n_blk + 1) * rows
    grid_spec = pltpu.PrefetchScalarGridSpec(
        num_scalar_prefetch=3,
        grid=(n_blk,),
        in_specs=[pl.BlockSpec(memory_space=pl.ANY),
                  pl.BlockSpec((1, d, f), lambda i, be, nb, tk: (be[i], 0, 0)),
                  pl.BlockSpec((1, d, f), lambda i, be, nb, tk: (be[i], 0, 0)),
                  pl.BlockSpec((1, f, d), lambda i, be, nb, tk: (be[i], 0, 0))],
        out_specs=pl.BlockSpec((rows, d), lambda i, be, nb, tk: (i, 0)),
        scratch_shapes=[pltpu.VMEM((2, rows, d), F32), pltpu.SemaphoreType.DMA((2,))],
    )
    return pl.pallas_call(
        _moe_kernel,
        grid_spec=grid_spec,
        out_shape=jax.ShapeDtypeStruct((n_blk * rows, d), F32),
        compiler_params=_params(("arbitrary",)),
        name="moe_ffn",
    )(blk_expert, n_used, row_tok, h1, wg, wu, wd)


def _combine_kernel(p0_ref, p1_ref, ys_hbm, h_ref, gt_ref, g2_ref, b2_ref, o_ref, ybuf, sem):
    i = pl.program_id(0)
    n = pl.num_programs(0)
    slot = i % 2
    tm = h_ref.shape[0]

    def issue(step, s):
        _row_gather(ys_hbm, p0_ref, step * tm, ybuf.at[s, 0], sem.at[s], tm)
        _row_gather(ys_hbm, p1_ref, step * tm, ybuf.at[s, 1], sem.at[s], tm)

    def drain(s):
        _row_gather_wait(ys_hbm, ybuf.at[s, 0], sem.at[s], tm)
        _row_gather_wait(ys_hbm, ybuf.at[s, 1], sem.at[s], tm)

    @pl.when(i == 0)
    def _():
        issue(0, 0)

    issue(i + 1, 1 - slot)
    drain(slot)
    gt = gt_ref[...]
    moe = gt[:, 0:1] * ybuf[slot, 0] + gt[:, 1:2] * ybuf[slot, 1]
    o_ref[...] = _layer_norm(DEEPNORM_ALPHA * h_ref[...] + moe, g2_ref[...], b2_ref[...])

    @pl.when(i == n - 1)
    def _():
        drain(1 - slot)


def _combine(pos0, pos1, ys, h1, gates, g2, b2, tm=256):
    t, d = h1.shape
    assert pos0.shape[0] == t + tm and pos1.shape[0] == t + tm
    const = lambda i, a, b: (0, 0)
    grid_spec = pltpu.PrefetchScalarGridSpec(
        num_scalar_prefetch=2,
        grid=(t // tm,),
        in_specs=[pl.BlockSpec(memory_space=pl.ANY),
                  pl.BlockSpec((tm, d), lambda i, a, b: (i, 0)),
                  pl.BlockSpec((tm, LANES), lambda i, a, b: (i, 0)),
                  pl.BlockSpec((1, d), const), pl.BlockSpec((1, d), const)],
        out_specs=pl.BlockSpec((tm, d), lambda i, a, b: (i, 0)),
        scratch_shapes=[pltpu.VMEM((2, 2, tm, d), F32), pltpu.SemaphoreType.DMA((2,))],
    )
    return pl.pallas_call(
        _combine_kernel,
        grid_spec=grid_spec,
        out_shape=jax.ShapeDtypeStruct((t, d), F32),
        compiler_params=_params(("arbitrary",)),
        name="moe_combine_ln2",
    )(pos0, pos1, ys, h1, gates, g2, b2)


def _rope_tables(seq):
    rows = seq // GRID_W
    row_idx = jnp.repeat(jnp.arange(rows, dtype=jnp.int32), GRID_W).astype(F32)
    col_idx = jnp.tile(jnp.arange(GRID_W, dtype=jnp.int32), rows).astype(F32)
    half = HEAD_DIM // 2
    inv_freq = ROPE_THETA ** (-jnp.arange(0, half, 2, dtype=F32) / half)
    ang_r = row_idx[:, None] * inv_freq[None, :]
    ang_c = col_idx[:, None] * inv_freq[None, :]
    cr, sr, cc, sc = jnp.cos(ang_r), jnp.sin(ang_r), jnp.cos(ang_c), jnp.sin(ang_c)
    cs = jnp.concatenate([cr, cc, cr, cc], axis=-1)
    sn = jnp.concatenate([-sr, -sc, sr, sc], axis=-1)
    return cs, sn


def _pair_split(v):
    lead = v.shape[:-1]
    q4 = v.reshape(lead + (-1, 2, 2, HEAD_DIM // 4))
    return jnp.swapaxes(q4, -3, -2).reshape(v.shape)


def _filter_features(seq):
    n1 = 2 * seq // DFT_MINOR
    n = (DFT_MINOR * np.arange(n1)[None, :] + np.arange(DFT_MINOR)[:, None]).astype(np.int64)
    j = np.where(n <= seq, n, 2 * seq - n)
    j = np.where(n == seq, 0, j).astype(np.float64)
    t = (j / (seq - 1))[..., None]
    w = (2.0 * math.pi * j / seq)[..., None]
    bands = np.linspace(1e-4, FILTER_BANDS - 1, FILTER_BANDS, dtype=np.float32).astype(np.float64)
    feats = np.zeros(n.shape + (LANES,), np.float32)
    feats[..., 0:1] = t
    feats[..., 1:1 + FILTER_BANDS] = np.cos(bands * w)
    feats[..., 1 + FILTER_BANDS:FILTER_EMB] = -np.sin(bands * w)
    feats[..., LANES - 1] = (n != seq)
    return jnp.asarray(feats)


def _dispatch_plan(ids, counts, n_tok, rows, tail):
    counts = counts.astype(jnp.int32)
    padded = (counts + rows - 1) // rows * rows
    pad_end = jnp.cumsum(padded)
    pad_start = pad_end - padded
    n_blk = (n_tok * TOP_K + N_EXPERTS * rows) // rows + 1
    blk_start = jnp.arange(n_blk, dtype=jnp.int32) * rows
    blk_expert = jnp.minimum(jnp.sum((pad_end[None, :] <= blk_start[:, None]).astype(jnp.int32), axis=1),
                             N_EXPERTS - 1)
    n_used = (pad_end[-1] // rows).astype(jnp.int32).reshape(1)
    experts = jnp.arange(N_EXPERTS, dtype=jnp.int32)[None, :]

    def dest(e, rank):
        start = jnp.sum(jnp.where(e[:, None] == experts, pad_start[None, :], 0), axis=1)
        return jnp.pad(start + rank, (0, tail))

    pos0 = dest(ids[:, 0], ids[:, 2])
    pos1 = dest(ids[:, 1], ids[:, 3])
    n_map = (n_blk + 1) * rows
    pad_lo = pad_start + counts
    pad_hi = pad_end.at[N_EXPERTS - 1].set(n_map)
    row_tok = _invert(pos0[:n_tok], pos1[:n_tok], pad_lo, pad_hi, n_map)
    return blk_expert, n_used, row_tok, pos0, pos1


def kernel(x, ln_in_g, ln_in_b, w_in, b_gate, q_norm_g, k_norm_g, hy_conv_w, hy_conv_b, filt_w1, filt_b1, filt_f1, filt_w2, filt_b2, filt_f2, filt_w3, hy_bias_d, w_attn_o, w_hy_o, w_out, ln1_g, ln1_b, w_route_grp, b_route_grp, w_route_exp, b_route_exp, w_exp_gate, w_exp_up, w_exp_down, ln2_g, ln2_b):
    batch, seq, d = x.shape
    assert batch == 2, "the long convolution packs exactly two batch rows as one complex signal"
    t = batch * seq
    hw = hy_bias_d.shape[1]
    l = 0
    x2 = x.reshape(t, d)
    row = lambda v: v.reshape(1, -1)

    cs, sn = _rope_tables(seq)
    qg = row(q_norm_g[l]) * (HEAD_DIM ** -0.5 * math.log2(math.e))
    qg = _pair_split(qg)
    kg = _pair_split(row(k_norm_g[l]))
    n_qk = ATTN_WIDTH + KV_WIDTH
    w_bf = jnp.concatenate([_pair_split(w_in[l][:, :n_qk]).astype(BF16), w_in[l][:, n_qk:].astype(BF16)], axis=1)
    proj = _ln_inproj(x2, row(ln_in_g), row(ln_in_b), w_bf, qg, kg, cs, sn, seq)

    attn = _attention(proj, batch, seq)

    hy_col0 = ATTN_WIDTH + 2 * KV_WIDTH
    z, x0c = _hy_pre(proj, hy_conv_w[l], row(hy_conv_b[l]), batch, seq, hw, hy_col0)
    n1 = 2 * seq // DFT_MINOR
    tab_sig, tab_out, tab_filt, mf, mi = _dft_tables(n1, n1 // 2)
    feats = _filter_features(seq)
    w1p = jnp.pad(filt_w1[l], ((0, LANES - FILTER_EMB), (0, 0)))
    min_decay = math.log(DECAY_TARGET) / SLOW_DECAY_PCT
    max_decay = math.log(DECAY_TARGET) / FAST_DECAY_PCT
    deltas = jnp.linspace(min_decay, max_decay, hw, dtype=F32)[None, :]
    two_sided, sumsq = _filter_time(feats, w1p, row(filt_b1[l]), row(filt_f1[l]), filt_w2[l], row(filt_b2[l]),
                                    row(filt_f2[l]), filt_w3[l], deltas)
    fa = _dft_stage_a(two_sided, tab_filt, 2 * n1, t2_major=True)
    hspec = _filter_stage_b(fa, mf, sumsq, n1)
    za = _dft_stage_a(z.reshape(batch, n1 // 2, DFT_MINOR * hw), tab_sig, 2 * n1)
    zb = _conv_stage_b(za, mf, mi, hspec, n1)
    yconv = _conv_stage_out(zb, tab_out, batch, n1).reshape(t, hw)

    gate_col0 = hy_col0 + 3 * hw
    u = _merge_a(attn, yconv, z, x0c, row(hy_bias_d[l]), proj, row(b_gate[l]), w_attn_o[l].astype(BF16),
                 w_hy_o[l].astype(BF16), gate_col0)
    n_r = N_GROUPS + N_EXPERTS
    wr = jnp.pad(jnp.concatenate([w_route_grp[l], w_route_exp[l]], axis=1), ((0, 0), (0, LANES - n_r)))
    br = jnp.pad(jnp.concatenate([b_route_grp[l], b_route_exp[l]]), (0, LANES - n_r)).reshape(1, LANES)
    h1, logits = _merge_b(u, x2, row(ln_in_g), row(ln_in_b), w_out[l].astype(BF16), row(ln1_g[l]), row(ln1_b[l]),
                          wr, br)

    ids, gates, cnt = _route(logits)
    counts = cnt[0, N_GROUPS:N_GROUPS + N_EXPERTS]
    combine_tile = 256
    blk_expert, n_used, row_tok, pos0, pos1 = _dispatch_plan(ids, counts, t, MOE_ROWS, combine_tile)
    ys = _moe_ffn(blk_expert, n_used, row_tok, h1, w_exp_gate[l].astype(BF16), w_exp_up[l].astype(BF16),
                  w_exp_down[l].astype(BF16))
    out = _combine(pos0, pos1, ys, h1, gates, row(ln2_g[l]), row(ln2_b[l]), tm=combine_tile)
    return out.reshape(batch, seq, d)
```
